```python
import jax, jax.numpy as jnp
from jax import lax
import numpy as np

D_MODEL = 1024
BATCH = 4
SEQ = 8192
DEPTH = 4

N_EVEN = (DEPTH + 1) // 2
N_ODD = DEPTH // 2
N_NORMS = 6

D_FF = ((8 * D_MODEL // 3 + 127) // 128) * 128
FFN_RES = 0.5

SB_HEADS = 8
SB_HEAD_DIM = 64
SB_WIDTH = SB_HEADS * SB_HEAD_DIM
SB_BLOCK = 128

CONV_CH = D_MODEL // 2
CONV_WIDTH = 31

EVEN_IN = 3 * SB_WIDTH + 2 * CONV_CH
EVEN_OUT = SB_WIDTH + CONV_CH

GLA_HEADS = 4
GLA_KEY = D_MODEL // 2
GLA_VAL = D_MODEL
GLA_DK = GLA_KEY // GLA_HEADS
GLA_DV = GLA_VAL // GLA_HEADS
GATE_RANK = 16
GATE_TAU = 16.0
GLA_CHUNK = 64
ODD_IN = 2 * GLA_KEY + 2 * GLA_VAL + GATE_RANK

NORM_EPS = 1e-6

kernel_name = "hybrid_sbattn_conformerconv_gla_macaron"


def rms_norm(x, g):
    xf = x.astype(jnp.float32)
    y = xf * lax.rsqrt(jnp.mean(xf * xf, axis=-1, keepdims=True) + NORM_EPS)
    return (y * g.astype(jnp.float32)).astype(x.dtype)


def layer_norm(x, g, b):
    xf = x.astype(jnp.float32)
    mu = jnp.mean(xf, axis=-1, keepdims=True)
    var = jnp.mean(jnp.square(xf - mu), axis=-1, keepdims=True)
    y = (xf - mu) * lax.rsqrt(var + NORM_EPS)
    return (y * g.astype(jnp.float32) + b.astype(jnp.float32)).astype(x.dtype)


def swiglu(x, w_in, w_out):
    a, b = jnp.split(x @ w_in, 2, axis=-1)
    return (jax.nn.silu(a) * b) @ w_out


def split_heads(t, n):
    B, S, W = t.shape
    return t.reshape(B, S, n, W // n).transpose(0, 2, 1, 3)


def merge_heads(t):
    B, H, S, d = t.shape
    return t.transpose(0, 2, 1, 3).reshape(B, S, H * d)


def stick_breaking_attention(q, k, v):
    B, H, S, Dh = q.shape
    scale = Dh ** -0.5
    outs = []
    for i in range(S // SB_BLOCK):
        t0, t1 = i * SB_BLOCK, (i + 1) * SB_BLOCK
        qb = q[:, :, t0:t1]
        kb = k[:, :, :t1]
        vb = v[:, :, :t1]
        z = jnp.einsum('bhqd,bhkd->bhqk', qb, kb).astype(jnp.float32) * scale
        mask = jnp.arange(t1)[None, :] < jnp.arange(t0, t1)[:, None]
        log_1m = jnp.where(mask, jax.nn.log_sigmoid(-z), 0.0)
        rest = lax.cumsum(log_1m, axis=3, reverse=True) - log_1m
        w = jnp.where(mask, jnp.exp(jax.nn.log_sigmoid(z) + rest), 0.0)
        outs.append(jnp.einsum('bhqk,bhkd->bhqd', w.astype(vb.dtype), vb))
    return jnp.concatenate(outs, axis=2)


def causal_depthwise_conv(x, w, b):
    K, C = w.shape
    xp = jnp.pad(x, ((0, 0), (K - 1, 0), (0, 0)))
    y = lax.conv_general_dilated(
        xp, w[:, None, :].astype(x.dtype), window_strides=(1,), padding='VALID',
        dimension_numbers=('NWC', 'WIO', 'NWC'), feature_group_count=C)
    return y + b.astype(x.dtype)


def even_mixer(h, w_in, w_out, dw_w, dw_b, ln_g, ln_b):
    proj = h @ w_in
    q, k, v, u, gate = jnp.split(
        proj, [SB_WIDTH, 2 * SB_WIDTH, 3 * SB_WIDTH, 3 * SB_WIDTH + CONV_CH], axis=-1)
    a_out = merge_heads(stick_breaking_attention(
        split_heads(q, SB_HEADS), split_heads(k, SB_HEADS), split_heads(v, SB_HEADS)))
    c = u * jax.nn.sigmoid(gate)
    c = causal_depthwise_conv(c, dw_w, dw_b)
    c = jax.nn.silu(layer_norm(c, ln_g, ln_b))
    return jnp.concatenate([a_out.astype(h.dtype), c], axis=-1) @ w_out


def gla_chunked(q, k, v, g):
    B, H, S, dk = q.shape
    dv = v.shape[-1]
    N = S // GLA_CHUNK

    def to_chunks(t):
        return t.reshape(B, H, N, GLA_CHUNK, t.shape[-1]).transpose(2, 0, 1, 3, 4)

    mask = jnp.tril(jnp.ones((GLA_CHUNK, GLA_CHUNK), dtype=bool))

    def step(state, inp):
        qc, kc, vc, gc = inp
        b = jnp.cumsum(gc, axis=2)
        inter = jnp.einsum('bhtk,bhkv->bhtv', qc * jnp.exp(b), state)
        diff = b[:, :, :, None, :] - b[:, :, None, :, :]
        decay = jnp.exp(jnp.where(mask[:, :, None], diff, -jnp.inf))
        scores = jnp.einsum('bhtk,bhsk,bhtsk->bhts', qc, kc, decay)
        out = inter + jnp.einsum('bhts,bhsv->bhtv', scores, vc)
        b_last = b[:, :, -1, :]
        state = jnp.exp(b_last)[..., None] * state + jnp.einsum(
            'bhsk,bhsv->bhkv', kc * jnp.exp(b_last[:, :, None, :] - b), vc)
        return state, out

    init = jnp.zeros((B, H, dk, dv), jnp.float32)
    _, o = lax.scan(step, init, (to_chunks(q), to_chunks(k), to_chunks(v), to_chunks(g)))
    return o.transpose(1, 2, 0, 3, 4).reshape(B, H, S, dv)


def odd_mixer(h, w_in, w_gate, b_gate, norm_g, w_out):
    proj = h @ w_in
    q, k, v, r, g_lr = jnp.split(
        proj, [GLA_KEY, 2 * GLA_KEY, 2 * GLA_KEY + GLA_VAL, 2 * GLA_KEY + 2 * GLA_VAL], axis=-1)
    g = jax.nn.log_sigmoid((g_lr @ w_gate + b_gate).astype(jnp.float32)) / GATE_TAU
    f32 = jnp.float32
    qh = split_heads(q, GLA_HEADS).astype(f32) * (GLA_DK ** -0.5)
    kh = split_heads(k, GLA_HEADS).astype(f32)
    vh = split_heads(v, GLA_HEADS).astype(f32)
    gh = split_heads(g, GLA_HEADS)
    o = gla_chunked(qh, kh, vh, gh)
    o = merge_heads(rms_norm(o, norm_g)).astype(h.dtype)
    return (o * jax.nn.silu(r)) @ w_out


def setup_inputs(seed: int = 0) -> dict:
    key = jax.random.key(seed)
    ks = jax.random.split(key, 20)
    f32 = jnp.float32

    def w(k, shape, fan_in):
        return jax.random.normal(k, shape, f32) * (fan_in ** -0.5)

    return {
        "x": jax.random.normal(ks[0], (BATCH, SEQ, D_MODEL), f32),
        "norm_g": 1.0 + 0.05 * jax.random.normal(ks[1], (DEPTH, N_NORMS, D_MODEL), f32),
        "ffn1_w_in": w(ks[2], (DEPTH, D_MODEL, 2 * D_FF), D_MODEL),
        "ffn1_w_out": w(ks[3], (DEPTH, D_FF, D_MODEL), D_FF),
        "ffn2_w_in": w(ks[4], (DEPTH, D_MODEL, 2 * D_FF), D_MODEL),
        "ffn2_w_out": w(ks[5], (DEPTH, D_FF, D_MODEL), D_FF),
        "hyb_w_in": w(ks[6], (N_EVEN, D_MODEL, EVEN_IN), D_MODEL),
        "hyb_w_out": w(ks[7], (N_EVEN, EVEN_OUT, D_MODEL), EVEN_OUT),
        "conv_dw_w": w(ks[8], (N_EVEN, CONV_WIDTH, CONV_CH), CONV_WIDTH),
        "conv_dw_b": 0.02 * jax.random.normal(ks[9], (N_EVEN, CONV_CH), f32),
        "conv_ln_g": 1.0 + 0.05 * jax.random.normal(ks[10], (N_EVEN, CONV_CH), f32),
        "conv_ln_b": 0.02 * jax.random.normal(ks[11], (N_EVEN, CONV_CH), f32),
        "gla_w_in": w(ks[12], (N_ODD, D_MODEL, ODD_IN), D_MODEL),
        "gla_w_gate": w(ks[13], (N_ODD, GATE_RANK, GLA_KEY), GATE_RANK),
        "gla_b_gate": 0.02 * jax.random.normal(ks[14], (N_ODD, GLA_KEY), f32),
        "gla_norm_g": 1.0 + 0.05 * jax.random.normal(ks[15], (N_ODD, GLA_DV), f32),
        "gla_w_out": w(ks[16], (N_ODD, GLA_VAL, D_MODEL), GLA_VAL),
    }


def reference(x, norm_g, ffn1_w_in, ffn1_w_out, ffn2_w_in, ffn2_w_out,
              hyb_w_in, hyb_w_out, conv_dw_w, conv_dw_b, conv_ln_g, conv_ln_b,
              gla_w_in, gla_w_gate, gla_b_gate, gla_norm_g, gla_w_out):
    h = x
    for l in range(DEPTH):
        g = norm_g[l]
        f = swiglu(rms_norm(h, g[0]), ffn1_w_in[l], ffn1_w_out[l])
        h = h + FFN_RES * rms_norm(f, g[1])
        hn = rms_norm(h, g[2])
        if l % 2 == 0:
            e = l // 2
            m = even_mixer(hn, hyb_w_in[e], hyb_w_out[e], conv_dw_w[e], conv_dw_b[e],
                           conv_ln_g[e], conv_ln_b[e])
        else:
            o = l // 2
            m = odd_mixer(hn, gla_w_in[o], gla_w_gate[o], gla_b_gate[o], gla_norm_g[o],
                          gla_w_out[o])
        h = h + rms_norm(m, g[3])
        f = swiglu(rms_norm(h, g[4]), ffn2_w_in[l], ffn2_w_out[l])
        h = h + FFN_RES * rms_norm(f, g[5])
    return h
```

```python
import functools

import jax
import jax.numpy as jnp
from jax import lax
from jax.experimental import pallas as pl
from jax.experimental.pallas import tpu as pltpu

F32 = jnp.float32
BF16 = jnp.bfloat16

NORM_EPS = 1e-6
FFN_RES = 0.5
SB_HEADS = 8
SB_HEAD_DIM = 64
CONV_WIDTH = 31
GLA_HEADS = 4
GATE_TAU = 16.0

LANES = 128
CONV_HALO = 32
GLA_SAFE_EXP = 80.0
VMEM_LIMIT = 56 * 1024 * 1024


def _params(*sem):
    return pltpu.CompilerParams(dimension_semantics=sem, vmem_limit_bytes=VMEM_LIMIT)


def _const_spec(shape):
    zeros = (0,) * len(shape)
    return pl.BlockSpec(shape, lambda *_: zeros, pipeline_mode=pl.Buffered(1))


def _rms(x, g):
    return x * lax.rsqrt(jnp.mean(x * x, axis=-1, keepdims=True) + NORM_EPS) * g


def _sigmoid(x):
    return 1.0 / (1.0 + jnp.exp(-x))


def _dot(a, b):
    return jnp.dot(a, b, preferred_element_type=F32)


def _dot_nt(a, b):
    return lax.dot_general(a, b, (((1,), (1,)), ((), ())), preferred_element_type=F32)


def _dot_tn(a, b):
    return lax.dot_general(a, b, (((0,), (0,)), ((), ())), preferred_element_type=F32)


def _ffn_kernel(h_ref, gpre_ref, gpost_ref, win_ref, wout_ref, o_ref, *, d_ff, n_chunks):
    h = h_ref[...]
    xn = _rms(h, gpre_ref[...]).astype(BF16)
    tf = d_ff // n_chunks
    f = None
    for c in range(n_chunks):
        a = _dot(xn, win_ref[:, c * tf:(c + 1) * tf])
        b = _dot(xn, win_ref[:, d_ff + c * tf:d_ff + (c + 1) * tf])
        s = (a * _sigmoid(a) * b).astype(BF16)
        part = _dot(s, wout_ref[c * tf:(c + 1) * tf, :])
        f = part if f is None else f + part
    o_ref[...] = h + FFN_RES * _rms(f, gpost_ref[...])


def _ffn(h, g_pre, g_post, w_in, w_out, *, tm=512, n_chunks=2):
    m, d = h.shape
    d_ff = w_out.shape[0]
    return pl.pallas_call(
        functools.partial(_ffn_kernel, d_ff=d_ff, n_chunks=n_chunks),
        grid=(m // tm,),
        in_specs=[
            pl.BlockSpec((tm, d), lambda i: (i, 0)),
            _const_spec((1, d)),
            _const_spec((1, d)),
            _const_spec((d, 2 * d_ff)),
            _const_spec((d_ff, d)),
        ],
        out_specs=pl.BlockSpec((tm, d), lambda i: (i, 0)),
        out_shape=jax.ShapeDtypeStruct((m, d), F32),
        compiler_params=_params("parallel"),
        name="ffn",
    )(h, g_pre, g_post, w_in, w_out)


def _even_proj_kernel(h_ref, g_ref, w_ref, qkv_ref, c_ref, *, sb_width, conv_ch):
    hn = _rms(h_ref[...], g_ref[...]).astype(BF16)
    proj = _dot(hn, w_ref[...])
    scale = SB_HEAD_DIM ** -0.5
    qkv_ref[:, :sb_width] = (proj[:, :sb_width] * scale).astype(BF16)
    qkv_ref[:, sb_width:] = proj[:, sb_width:3 * sb_width].astype(BF16)
    u = proj[:, 3 * sb_width:3 * sb_width + conv_ch]
    gate = proj[:, 3 * sb_width + conv_ch:]
    c_ref[...] = u * _sigmoid(gate)


def _even_proj(h, g, w_in, *, sb_width, conv_ch, tm=512):
    m, d = h.shape
    n = w_in.shape[1]
    return pl.pallas_call(
        functools.partial(_even_proj_kernel, sb_width=sb_width, conv_ch=conv_ch),
        grid=(m // tm,),
        in_specs=[
            pl.BlockSpec((tm, d), lambda i: (i, 0)),
            _const_spec((1, d)),
            _const_spec((d, n)),
        ],
        out_specs=[
            pl.BlockSpec((tm, 3 * sb_width), lambda i: (i, 0)),
            pl.BlockSpec((tm, conv_ch), lambda i: (i, 0)),
        ],
        out_shape=[
            jax.ShapeDtypeStruct((m, 3 * sb_width), BF16),
            jax.ShapeDtypeStruct((m, conv_ch), F32),
        ],
        compiler_params=_params("parallel"),
        name="even_proj",
    )(h, g, w_in)


def _sb_attn_kernel(q_ref, k_ref, v_ref, u_ref, o_ref):
    t = q_ref.shape[1]
    i = pl.program_id(2)
    q = q_ref[0]
    lane = lax.broadcasted_iota(jnp.int32, (t, LANES), 1)
    row = lax.broadcasted_iota(jnp.int32, (t, t), 0)
    col = lax.broadcasted_iota(jnp.int32, (t, t), 1)
    strictly_before = col < row
    first_head = lane < SB_HEAD_DIM
    zero = jnp.zeros_like(q)
    q_heads = (jnp.where(first_head, q, zero), jnp.where(first_head, zero, q))
    u = u_ref[...]

    def visit(j, state, diagonal):
        start = pl.multiple_of(j * t, t)
        kb = k_ref[0, pl.ds(start, t), :]
        vb = v_ref[0, pl.ds(start, t), :]
        new_state = []
        for hd in range(2):
            carry, acc = state[2 * hd], state[2 * hd + 1]
            z = _dot_nt(q_heads[hd], kb)
            sp = jnp.maximum(z, 0.0) + jnp.log(1.0 + jnp.exp(-jnp.abs(z)))
            if diagonal:
                sp = jnp.where(strictly_before, sp, 0.0)
            hi = sp.astype(BF16)
            lo = (sp - hi.astype(F32)).astype(BF16)
            cs = _dot(hi, u) + _dot(lo, u)
            w = jnp.exp(z + cs[:, :t] + carry)
            if diagonal:
                w = jnp.where(strictly_before, w, 0.0)
            acc = acc + _dot(w.astype(BF16), vb)
            carry = carry + cs[:, t:]
            new_state += [carry, acc]
        return tuple(new_state)

    zeros = jnp.zeros((t, LANES), F32)
    state = visit(i, (zeros, zeros, zeros, zeros), True)
    state = lax.fori_loop(0, i, lambda jj, st: visit(i - 1 - jj, st, False), state)
    o_ref[0] = jnp.where(first_head, state[1], state[3]).astype(o_ref.dtype)


def _sb_attention(qkv, *, sb_width, t=LANES):
    b, s, _ = qkv.shape
    pairs = sb_width // LANES
    tri = (jnp.arange(t)[:, None] >= jnp.arange(t)[None, :])
    u = -jnp.concatenate([tri.astype(F32), jnp.ones((t, LANES), F32)], axis=1).astype(BF16)
    return pl.pallas_call(
        _sb_attn_kernel,
        grid=(b, pairs, s // t),
        in_specs=[
            pl.BlockSpec((1, t, LANES), lambda bi, p, i: (bi, i, p)),
            pl.BlockSpec((1, s, LANES), lambda bi, p, i: (bi, 0, pairs + p)),
            pl.BlockSpec((1, s, LANES), lambda bi, p, i: (bi, 0, 2 * pairs + p)),
            _const_spec((t, t + LANES)),
        ],
        out_specs=pl.BlockSpec((1, t, LANES), lambda bi, p, i: (bi, i, p)),
        out_shape=jax.ShapeDtypeStruct((b, s, sb_width), BF16),
        compiler_params=_params("parallel", "parallel", "arbitrary"),
        name="sb_attention",
    )(qkv, qkv, qkv, u)


def _even_out_kernel(a_ref, c_ref, halo_ref, h_ref, dww_ref, dwb_ref, lng_ref, lnb_ref,
                     wout_ref, g_ref, o_ref, xbuf, ybuf, *, rows):
    t = c_ref.shape[1]
    ch = c_ref.shape[2]
    i = pl.program_id(1)
    halo = halo_ref[0]
    xbuf[0:CONV_HALO, :] = jnp.where(i > 0, halo, jnp.zeros_like(halo))
    xbuf[CONV_HALO:, :] = c_ref[0]
    first = CONV_HALO - (CONV_WIDTH - 1)
    for r0 in range(0, t, rows):
        for c0 in range(0, ch, LANES):
            acc = jnp.zeros((rows, LANES), F32) + dwb_ref[:, c0:c0 + LANES]
            for k in range(CONV_WIDTH):
                acc = acc + dww_ref[k:k + 1, c0:c0 + LANES] * xbuf[first + k + r0:first + k + r0 + rows, c0:c0 + LANES]
            ybuf[r0:r0 + rows, c0:c0 + LANES] = acc
    y = ybuf[...]
    mu = jnp.mean(y, axis=-1, keepdims=True)
    yc = y - mu
    var = jnp.mean(yc * yc, axis=-1, keepdims=True)
    yn = yc * lax.rsqrt(var + NORM_EPS) * lng_ref[...] + lnb_ref[...]
    cact = (yn * _sigmoid(yn)).astype(BF16)
    aw = a_ref.shape[2]
    m = _dot(a_ref[0], wout_ref[:aw, :]) + _dot(cact, wout_ref[aw:, :])
    o_ref[0] = h_ref[0] + _rms(m, g_ref[...])


def _even_out(a, c, h, dw_w, dw_b, ln_g, ln_b, w_out, g, *, t=256, rows=64):
    b, s, d = h.shape
    aw, ch = a.shape[2], c.shape[2]
    per = t // CONV_HALO
    return pl.pallas_call(
        functools.partial(_even_out_kernel, rows=rows),
        grid=(b, s // t),
        in_specs=[
            pl.BlockSpec((1, t, aw), lambda bi, i: (bi, i, 0)),
            pl.BlockSpec((1, t, ch), lambda bi, i: (bi, i, 0)),
            pl.BlockSpec((1, CONV_HALO, ch), lambda bi, i: (bi, jnp.maximum(i * per - 1, 0), 0)),
            pl.BlockSpec((1, t, d), lambda bi, i: (bi, i, 0)),
            _const_spec((CONV_WIDTH, ch)),
            _const_spec((1, ch)),
            _const_spec((1, ch)),
            _const_spec((1, ch)),
            _const_spec((aw + ch, d)),
            _const_spec((1, d)),
        ],
        out_specs=pl.BlockSpec((1, t, d), lambda bi, i: (bi, i, 0)),
        out_shape=jax.ShapeDtypeStruct((b, s, d), F32),
        scratch_shapes=[pltpu.VMEM((t + CONV_HALO, ch), F32), pltpu.VMEM((t, ch), F32)],
        compiler_params=_params("parallel", "parallel"),
        name="even_out",
    )(a, c, c, h, dw_w, dw_b, ln_g, ln_b, w_out, g)


def _odd_proj_kernel(h_ref, g_ref, w_ref, wlr_ref, wgate_ref, bgate_ref,
                     q_ref, k_ref, v_ref, r_ref, lg_ref, *, key_w, val_w):
    hn = _rms(h_ref[...], g_ref[...]).astype(BF16)
    proj = _dot(hn, w_ref[...])
    dk = key_w // GLA_HEADS
    q_ref[...] = proj[:, :key_w] * (dk ** -0.5)
    k_ref[...] = proj[:, key_w:2 * key_w]
    v_ref[...] = proj[:, 2 * key_w:2 * key_w + val_w].astype(BF16)
    r_ref[...] = proj[:, 2 * key_w + val_w:]
    g_lr = _dot(hn, wlr_ref[...]).astype(BF16)
    x = _dot(g_lr, wgate_ref[...]) + bgate_ref[...]
    lg_ref[...] = -(jnp.maximum(-x, 0.0) + jnp.log(1.0 + jnp.exp(-jnp.abs(x)))) / GATE_TAU


def _odd_proj(h, g, w_main, w_lr, w_gate, b_gate, *, key_w, val_w, tm=512):
    m, d = h.shape
    rank = w_lr.shape[1]
    row = lambda i: (i, 0)
    return pl.pallas_call(
        functools.partial(_odd_proj_kernel, key_w=key_w, val_w=val_w),
        grid=(m // tm,),
        in_specs=[
            pl.BlockSpec((tm, d), row),
            _const_spec((1, d)),
            _const_spec((d, 2 * key_w + 2 * val_w)),
            _const_spec((d, rank)),
            _const_spec((rank, key_w)),
            _const_spec((1, key_w)),
        ],
        out_specs=[
            pl.BlockSpec((tm, key_w), row),
            pl.BlockSpec((tm, key_w), row),
            pl.BlockSpec((tm, val_w), row),
            pl.BlockSpec((tm, val_w), row),
            pl.BlockSpec((tm, key_w), row),
        ],
        out_shape=[
            jax.ShapeDtypeStruct((m, key_w), F32),
            jax.ShapeDtypeStruct((m, key_w), F32),
            jax.ShapeDtypeStruct((m, val_w), BF16),
            jax.ShapeDtypeStruct((m, val_w), F32),
            jax.ShapeDtypeStruct((m, key_w), F32),
        ],
        compiler_params=_params("parallel"),
        name="odd_proj",
    )(h, g, w_main, w_lr, w_gate, b_gate)


def _gla_kernel(q_ref, k_ref, v_ref, r_ref, lg_ref, tri_ref, ng_ref, o_ref,
                state_ref, sc_ref):
    c = q_ref.shape[1]
    key_w = q_ref.shape[2]
    dk = key_w // GLA_HEADS
    dv = v_ref.shape[2] // GLA_HEADS

    @pl.when(pl.program_id(1) == 0)
    def _():
        state_ref[...] = jnp.zeros_like(state_ref)

    lg = lg_ref[0]
    g1 = lg.astype(BF16)
    rem = lg - g1.astype(F32)
    g2 = rem.astype(BF16)
    g3 = (rem - g2.astype(F32)).astype(BF16)
    tri = tri_ref[...]
    b = _dot(tri, g1) + _dot(tri, g2) + _dot(tri, g3)
    mid = b[c // 2:c // 2 + 1, :]
    last = b[c - 1:c, :]
    safe = jnp.max(jnp.abs(b - mid)) < GLA_SAFE_EXP

    row = lax.broadcasted_iota(jnp.int32, (c, c), 0)
    col = lax.broadcasted_iota(jnp.int32, (c, c), 1)
    causal = col <= row
    row_k = lax.broadcasted_iota(jnp.int32, (c, dk), 0)

    for hd in range(GLA_HEADS):
        ks = slice(hd * dk, (hd + 1) * dk)
        vs = slice(hd * dv, (hd + 1) * dv)
        q = q_ref[0, :, ks]
        k = k_ref[0, :, ks]
        v = v_ref[0, :, vs]
        bh = b[:, ks]
        mh = mid[:, ks]
        lh = last[:, ks]

        @pl.when(safe)
        def _():
            qd = (q * jnp.exp(bh - mh)).astype(BF16)
            kd = (k * jnp.exp(mh - bh)).astype(BF16)
            sc_ref[...] = _dot_nt(qd, kd)

        @pl.when(jnp.logical_not(safe))
        def _():
            sc_ref[...] = jnp.zeros_like(sc_ref)

            def column(s, carry):
                is_s = row_k == s
                k_s = jnp.sum(jnp.where(is_s, k, 0.0), axis=0, keepdims=True)
                b_s = jnp.sum(jnp.where(is_s, bh, 0.0), axis=0, keepdims=True)
                decay = jnp.exp(jnp.minimum(bh - b_s, 0.0))
                val = jnp.sum(q * k_s * decay, axis=-1, keepdims=True)
                sc_ref[...] += jnp.where(col == s, val, 0.0)
                return carry

            lax.fori_loop(0, c, column, 0)

        scores = jnp.where(causal, sc_ref[...], 0.0).astype(BF16)
        st = state_ref[hd]
        q_in = (q * jnp.exp(bh)).astype(BF16)
        o = _dot(scores, v) + _dot_nt(q_in, st.astype(BF16))
        k_out = (k * jnp.exp(lh - bh)).astype(BF16)
        state_ref[hd] = st * jnp.exp(lh) + _dot_tn(v, k_out)
        on = _rms(o, ng_ref[...])
        r = r_ref[0, :, vs]
        o_ref[0, :, vs] = (on * (r * _sigmoid(r))).astype(o_ref.dtype)


def _gla(q, k, v, r, lg, norm_g, *, c=128):
    b, s, key_w = q.shape
    val_w = v.shape[2]
    dk, dv = key_w // GLA_HEADS, val_w // GLA_HEADS
    tri = (jnp.arange(c)[:, None] >= jnp.arange(c)[None, :]).astype(BF16)
    blk = lambda w: pl.BlockSpec((1, c, w), lambda bi, i: (bi, i, 0))
    return pl.pallas_call(
        _gla_kernel,
        grid=(b, s // c),
        in_specs=[blk(key_w), blk(key_w), blk(val_w), blk(val_w), blk(key_w),
                  _const_spec((c, c)), _const_spec((1, dv))],
        out_specs=blk(val_w),
        out_shape=jax.ShapeDtypeStruct((b, s, val_w), BF16),
        scratch_shapes=[
            pltpu.VMEM((GLA_HEADS, dv, dk), F32),
            pltpu.VMEM((c, c), F32),
        ],
        compiler_params=_params("parallel", "arbitrary"),
        name="gla",
    )(q, k, v, r, lg, tri, norm_g)


def _odd_out_kernel(y_ref, h_ref, w_ref, g_ref, o_ref):
    m = _dot(y_ref[...], w_ref[...])
    o_ref[...] = h_ref[...] + _rms(m, g_ref[...])


def _odd_out(y, h, w_out, g, *, tm=512):
    m, d = h.shape
    n = y.shape[1]
    return pl.pallas_call(
        _odd_out_kernel,
        grid=(m // tm,),
        in_specs=[
            pl.BlockSpec((tm, n), lambda i: (i, 0)),
            pl.BlockSpec((tm, d), lambda i: (i, 0)),
            _const_spec((n, d)),
            _const_spec((1, d)),
        ],
        out_specs=pl.BlockSpec((tm, d), lambda i: (i, 0)),
        out_shape=jax.ShapeDtypeStruct((m, d), F32),
        compiler_params=_params("parallel"),
        name="odd_out",
    )(y, h, w_out, g)


def kernel(x, norm_g, ffn1_w_in, ffn1_w_out, ffn2_w_in, ffn2_w_out, hyb_w_in, hyb_w_out,
           conv_dw_w, conv_dw_b, conv_ln_g, conv_ln_b, gla_w_in, gla_w_gate, gla_b_gate,
           gla_norm_g, gla_w_out):
    b, s, d = x.shape
    m = b * s
    depth = norm_g.shape[0]
    sb_width = SB_HEADS * SB_HEAD_DIM
    conv_ch = conv_dw_w.shape[2]
    key_w = gla_w_gate.shape[2]
    val_w = gla_w_out.shape[1]

    h = x.reshape(m, d)
    for l in range(depth):
        g = norm_g[l][:, None, :]
        h = _ffn(h, g[0], g[1], ffn1_w_in[l].astype(BF16), ffn1_w_out[l].astype(BF16))
        if l % 2 == 0:
            e = l // 2
            qkv, cglu = _even_proj(h, g[2], hyb_w_in[e].astype(BF16), sb_width=sb_width, conv_ch=conv_ch)
            a = _sb_attention(qkv.reshape(b, s, 3 * sb_width), sb_width=sb_width)
            h = _even_out(a, cglu.reshape(b, s, conv_ch), h.reshape(b, s, d), conv_dw_w[e],
                          conv_dw_b[e][None, :], conv_ln_g[e][None, :], conv_ln_b[e][None, :],
                          hyb_w_out[e].astype(BF16), g[3]).reshape(m, d)
        else:
            o = l // 2
            n_main = 2 * key_w + 2 * val_w
            w_in = gla_w_in[o]
            q, k, v, r, lg = _odd_proj(
                h, g[2], w_in[:, :n_main].astype(BF16), w_in[:, n_main:].astype(BF16),
                gla_w_gate[o].astype(BF16), gla_b_gate[o][None, :], key_w=key_w, val_w=val_w)
            to3 = lambda a_: a_.reshape(b, s, a_.shape[-1])
            y = _gla(to3(q), to3(k), to3(v), to3(r), to3(lg), gla_norm_g[o][None, :])
            h = _odd_out(y.reshape(m, val_w), h, gla_w_out[o].astype(BF16), g[3])
        h = _ffn(h, g[4], g[5], ffn2_w_in[l].astype(BF16), ffn2_w_out[l].astype(BF16))
    return h.reshape(b, s, d)
```

```python
import functools

import jax
import jax.numpy as jnp
from jax import lax
from jax.experimental import pallas as pl
from jax.experimental.pallas import tpu as pltpu

F32 = jnp.float32
BF16 = jnp.bfloat16

NORM_EPS = 1e-6
FFN_RES = 0.5
SB_HEADS = 8
SB_HEAD_DIM = 64
CONV_WIDTH = 31
GLA_HEADS = 4
GATE_TAU = 16.0

LANES = 128
SB_TK = 128
SB_MASKED = -1e30
CONV_HALO = 32
GLA_SAFE_EXP = 80.0
VMEM_LIMIT = 56 * 1024 * 1024


def _params(*sem):
    return pltpu.CompilerParams(dimension_semantics=sem, vmem_limit_bytes=VMEM_LIMIT)


def _const_spec(shape):
    zeros = (0,) * len(shape)
    return pl.BlockSpec(shape, lambda *_: zeros, pipeline_mode=pl.Buffered(1))


def _rms(x, g):
    return x * lax.rsqrt(jnp.mean(x * x, axis=-1, keepdims=True) + NORM_EPS) * g


def _sigmoid(x):
    return 1.0 / (1.0 + jnp.exp(-x))


def _dot(a, b):
    return jnp.dot(a, b, preferred_element_type=F32)


def _dot_nt(a, b):
    return lax.dot_general(a, b, (((1,), (1,)), ((), ())), preferred_element_type=F32)


def _dot_tn(a, b):
    return lax.dot_general(a, b, (((0,), (0,)), ((), ())), preferred_element_type=F32)


def _ffn_kernel(h_ref, gpre_ref, gpost_ref, win_ref, wout_ref, o_ref, *, d_ff, n_chunks):
    h = h_ref[...]
    xn = _rms(h, gpre_ref[...]).astype(BF16)
    tf = d_ff // n_chunks
    f = None
    for c in range(n_chunks):
        a = _dot(xn, win_ref[:, c * tf:(c + 1) * tf])
        b = _dot(xn, win_ref[:, d_ff + c * tf:d_ff + (c + 1) * tf])
        s = (a * _sigmoid(a) * b).astype(BF16)
        part = _dot(s, wout_ref[c * tf:(c + 1) * tf, :])
        f = part if f is None else f + part
    o_ref[...] = h + FFN_RES * _rms(f, gpost_ref[...])


def _ffn(h, g_pre, g_post, w_in, w_out, *, tm=512, n_chunks=2):
    m, d = h.shape
    d_ff = w_out.shape[0]
    return pl.pallas_call(
        functools.partial(_ffn_kernel, d_ff=d_ff, n_chunks=n_chunks),
        grid=(m // tm,),
        in_specs=[
            pl.BlockSpec((tm, d), lambda i: (i, 0)),
            _const_spec((1, d)),
            _const_spec((1, d)),
            _const_spec((d, 2 * d_ff)),
            _const_spec((d_ff, d)),
        ],
        out_specs=pl.BlockSpec((tm, d), lambda i: (i, 0)),
        out_shape=jax.ShapeDtypeStruct((m, d), F32),
        compiler_params=_params("parallel"),
        name="ffn",
    )(h, g_pre, g_post, w_in, w_out)


def _even_proj_kernel(h_ref, g_ref, w_ref, qkv_ref, c_ref, *, sb_width, conv_ch):
    hn = _rms(h_ref[...], g_ref[...]).astype(BF16)
    proj = _dot(hn, w_ref[...])
    scale = SB_HEAD_DIM ** -0.5
    qkv_ref[:, :sb_width] = (proj[:, :sb_width] * scale).astype(BF16)
    qkv_ref[:, sb_width:] = proj[:, sb_width:3 * sb_width].astype(BF16)
    u = proj[:, 3 * sb_width:3 * sb_width + conv_ch]
    gate = proj[:, 3 * sb_width + conv_ch:]
    c_ref[...] = u * _sigmoid(gate)


def _even_proj(h, g, w_in, *, sb_width, conv_ch, tm=512):
    m, d = h.shape
    n = w_in.shape[1]
    return pl.pallas_call(
        functools.partial(_even_proj_kernel, sb_width=sb_width, conv_ch=conv_ch),
        grid=(m // tm,),
        in_specs=[
            pl.BlockSpec((tm, d), lambda i: (i, 0)),
            _const_spec((1, d)),
            _const_spec((d, n)),
        ],
        out_specs=[
            pl.BlockSpec((tm, 3 * sb_width), lambda i: (i, 0)),
            pl.BlockSpec((tm, conv_ch), lambda i: (i, 0)),
        ],
        out_shape=[
            jax.ShapeDtypeStruct((m, 3 * sb_width), BF16),
            jax.ShapeDtypeStruct((m, conv_ch), F32),
        ],
        compiler_params=_params("parallel"),
        name="even_proj",
    )(h, g, w_in)


def _sb_attn_kernel(q_ref, k_ref, v_ref, bd_ref, o_ref, zs, hs, ls, ws, carry, acc):
    tq = q_ref.shape[1]
    tk = SB_TK
    nband = tq // tk
    i = pl.program_id(2)
    q0 = pl.multiple_of(i * tq, tq)
    q = q_ref[0]
    first_head = lax.broadcasted_iota(jnp.int32, (tk, LANES), 1) < SB_HEAD_DIM
    row = lax.broadcasted_iota(jnp.int32, (tq, 2 * tk), 0)
    key_in_block = lax.broadcasted_iota(jnp.int32, (tq, 2 * tk), 1) & (tk - 1)

    def key_start(n):
        return pl.multiple_of(q0 + (nband - 1 - n) * tk, tk)

    def stack_heads(blk):
        zero = jnp.zeros_like(blk)
        return jnp.concatenate([jnp.where(first_head, blk, zero), jnp.where(first_head, zero, blk)], axis=0)

    def stage_a(n, slot, band):
        z = _dot_nt(q, stack_heads(k_ref[0, pl.ds(key_start(n), tk), :]))
        sp = jnp.maximum(z, 0.0) + jnp.log(1.0 + jnp.exp(-jnp.abs(z)))
        if band:
            allowed = key_in_block + (nband - 1 - n) * tk < row
            sp = jnp.where(allowed, sp, 0.0)
            z = jnp.where(allowed, z, SB_MASKED)
        hi = sp.astype(BF16)
        zs[slot] = z
        hs[slot] = hi
        ls[slot] = (sp - hi.astype(F32)).astype(BF16)

    def stage_b(slot_in, slot_out):
        bd = bd_ref[...]
        cum = _dot(hs[slot_in], bd) + _dot(ls[slot_in], bd)
        c = carry[...]
        ws[slot_out] = jnp.exp(zs[slot_in] + cum + c).astype(BF16)
        total = jnp.concatenate([jnp.broadcast_to(cum[:, 0:1], (tq, tk)),
                                 jnp.broadcast_to(cum[:, tk:tk + 1], (tq, tk))], axis=1)
        carry[...] = c + total

    def stage_c(n, slot):
        acc[...] += _dot(ws[slot], stack_heads(v_ref[0, pl.ds(key_start(n), tk), :]))

    carry[...] = jnp.zeros_like(carry)
    acc[...] = jnp.zeros_like(acc)
    for n in range(nband):
        if n >= 2:
            stage_c(n - 2, (n - 1) % 2)
        if n >= 1:
            stage_b((n - 1) % 2, n % 2)
        stage_a(n, n % 2, True)

    def two_steps(m, _):
        n = nband + 2 * m
        stage_c(n - 2, 1)
        stage_b(1, 0)
        stage_a(n, 0, False)
        stage_c(n - 1, 0)
        stage_b(0, 1)
        stage_a(n + 1, 1, False)
        return 0

    lax.fori_loop(0, i * (nband // 2), two_steps, 0)
    last = (i + 1) * nband
    stage_c(last - 2, 1)
    stage_b(1, 0)
    stage_c(last - 1, 0)
    o_ref[0] = acc[...].astype(o_ref.dtype)


def _sb_attention(qkv, *, sb_width, tq=512):
    b, s, _ = qkv.shape
    tk = SB_TK
    assert tq % (2 * tk) == 0 and s % tq == 0
    pairs = sb_width // LANES
    tri = (jnp.arange(tk)[:, None] >= jnp.arange(tk)[None, :]).astype(F32)
    zero = jnp.zeros_like(tri)
    bd = -jnp.block([[tri, zero], [zero, tri]]).astype(BF16)
    return pl.pallas_call(
        _sb_attn_kernel,
        grid=(b, pairs, s // tq),
        in_specs=[
            pl.BlockSpec((1, tq, LANES), lambda bi, p, i: (bi, i, p)),
            pl.BlockSpec((1, s, LANES), lambda bi, p, i: (bi, 0, pairs + p)),
            pl.BlockSpec((1, s, LANES), lambda bi, p, i: (bi, 0, 2 * pairs + p)),
            _const_spec((2 * tk, 2 * tk)),
        ],
        out_specs=pl.BlockSpec((1, tq, LANES), lambda bi, p, i: (bi, i, p)),
        out_shape=jax.ShapeDtypeStruct((b, s, sb_width), BF16),
        scratch_shapes=[
            pltpu.VMEM((2, tq, 2 * tk), F32),
            pltpu.VMEM((2, tq, 2 * tk), BF16),
            pltpu.VMEM((2, tq, 2 * tk), BF16),
            pltpu.VMEM((2, tq, 2 * tk), BF16),
            pltpu.VMEM((tq, 2 * tk), F32),
            pltpu.VMEM((tq, LANES), F32),
        ],
        compiler_params=_params("parallel", "parallel", "arbitrary"),
        name="sb_attention",
    )(qkv, qkv, qkv, bd)


def _even_out_kernel(a_ref, c_ref, halo_ref, h_ref, dww_ref, dwb_ref, lng_ref, lnb_ref,
                     wout_ref, g_ref, o_ref, xbuf, ybuf, *, rows):
    t = c_ref.shape[1]
    ch = c_ref.shape[2]
    i = pl.program_id(1)
    halo = halo_ref[0]
    xbuf[0:CONV_HALO, :] = jnp.where(i > 0, halo, jnp.zeros_like(halo))
    xbuf[CONV_HALO:, :] = c_ref[0]
    first = CONV_HALO - (CONV_WIDTH - 1)
    for r0 in range(0, t, rows):
        for c0 in range(0, ch, LANES):
            acc = jnp.zeros((rows, LANES), F32) + dwb_ref[:, c0:c0 + LANES]
            for k in range(CONV_WIDTH):
                acc = acc + dww_ref[k:k + 1, c0:c0 + LANES] * xbuf[first + k + r0:first + k + r0 + rows, c0:c0 + LANES]
            ybuf[r0:r0 + rows, c0:c0 + LANES] = acc
    y = ybuf[...]
    mu = jnp.mean(y, axis=-1, keepdims=True)
    yc = y - mu
    var = jnp.mean(yc * yc, axis=-1, keepdims=True)
    yn = yc * lax.rsqrt(var + NORM_EPS) * lng_ref[...] + lnb_ref[...]
    cact = (yn * _sigmoid(yn)).astype(BF16)
    aw = a_ref.shape[2]
    m = _dot(a_ref[0], wout_ref[:aw, :]) + _dot(cact, wout_ref[aw:, :])
    o_ref[0] = h_ref[0] + _rms(m, g_ref[...])


def _even_out(a, c, h, dw_w, dw_b, ln_g, ln_b, w_out, g, *, t=256, rows=64):
    b, s, d = h.shape
    aw, ch = a.shape[2], c.shape[2]
    per = t // CONV_HALO
    return pl.pallas_call(
        functools.partial(_even_out_kernel, rows=rows),
        grid=(b, s // t),
        in_specs=[
            pl.BlockSpec((1, t, aw), lambda bi, i: (bi, i, 0)),
            pl.BlockSpec((1, t, ch), lambda bi, i: (bi, i, 0)),
            pl.BlockSpec((1, CONV_HALO, ch), lambda bi, i: (bi, jnp.maximum(i * per - 1, 0), 0)),
            pl.BlockSpec((1, t, d), lambda bi, i: (bi, i, 0)),
            _const_spec((CONV_WIDTH, ch)),
            _const_spec((1, ch)),
            _const_spec((1, ch)),
            _const_spec((1, ch)),
            _const_spec((aw + ch, d)),
            _const_spec((1, d)),
        ],
        out_specs=pl.BlockSpec((1, t, d), lambda bi, i: (bi, i, 0)),
        out_shape=jax.ShapeDtypeStruct((b, s, d), F32),
        scratch_shapes=[pltpu.VMEM((t + CONV_HALO, ch), F32), pltpu.VMEM((t, ch), F32)],
        compiler_params=_params("parallel", "parallel"),
        name="even_out",
    )(a, c, c, h, dw_w, dw_b, ln_g, ln_b, w_out, g)


def _odd_proj_kernel(h_ref, g_ref, w_ref, wlr_ref, wgate_ref, bgate_ref,
                     q_ref, k_ref, v_ref, r_ref, lg_ref, *, key_w, val_w):
    hn = _rms(h_ref[...], g_ref[...]).astype(BF16)
    proj = _dot(hn, w_ref[...])
    dk = key_w // GLA_HEADS
    q_ref[...] = proj[:, :key_w] * (dk ** -0.5)
    k_ref[...] = proj[:, key_w:2 * key_w]
    v_ref[...] = proj[:, 2 * key_w:2 * key_w + val_w].astype(BF16)
    r_ref[...] = proj[:, 2 * key_w + val_w:]
    g_lr = _dot(hn, wlr_ref[...]).astype(BF16)
    x = _dot(g_lr, wgate_ref[...]) + bgate_ref[...]
    lg_ref[...] = -(jnp.maximum(-x, 0.0) + jnp.log(1.0 + jnp.exp(-jnp.abs(x)))) / GATE_TAU


def _odd_proj(h, g, w_main, w_lr, w_gate, b_gate, *, key_w, val_w, tm=512):
    m, d = h.shape
    rank = w_lr.shape[1]
    row = lambda i: (i, 0)
    return pl.pallas_call(
        functools.partial(_odd_proj_kernel, key_w=key_w, val_w=val_w),
        grid=(m // tm,),
        in_specs=[
            pl.BlockSpec((tm, d), row),
            _const_spec((1, d)),
            _const_spec((d, 2 * key_w + 2 * val_w)),
            _const_spec((d, rank)),
            _const_spec((rank, key_w)),
            _const_spec((1, key_w)),
        ],
        out_specs=[
            pl.BlockSpec((tm, key_w), row),
            pl.BlockSpec((tm, key_w), row),
            pl.BlockSpec((tm, val_w), row),
            pl.BlockSpec((tm, val_w), row),
            pl.BlockSpec((tm, key_w), row),
        ],
        out_shape=[
            jax.ShapeDtypeStruct((m, key_w), F32),
            jax.ShapeDtypeStruct((m, key_w), F32),
            jax.ShapeDtypeStruct((m, val_w), BF16),
            jax.ShapeDtypeStruct((m, val_w), F32),
            jax.ShapeDtypeStruct((m, key_w), F32),
        ],
        compiler_params=_params("parallel"),
        name="odd_proj",
    )(h, g, w_main, w_lr, w_gate, b_gate)


def _gla_kernel(q_ref, k_ref, v_ref, r_ref, lg_ref, tri_ref, ng_ref, o_ref,
                state_ref, sc_ref):
    c = q_ref.shape[1]
    key_w = q_ref.shape[2]
    dk = key_w // GLA_HEADS
    dv = v_ref.shape[2] // GLA_HEADS

    @pl.when(pl.program_id(1) == 0)
    def _():
        state_ref[...] = jnp.zeros_like(state_ref)

    lg = lg_ref[0]
    g1 = lg.astype(BF16)
    rem = lg - g1.astype(F32)
    g2 = rem.astype(BF16)
    g3 = (rem - g2.astype(F32)).astype(BF16)
    tri = tri_ref[...]
    b = _dot(tri, g1) + _dot(tri, g2) + _dot(tri, g3)
    mid = b[c // 2:c // 2 + 1, :]
    last = b[c - 1:c, :]
    safe = jnp.max(jnp.abs(b - mid)) < GLA_SAFE_EXP

    row = lax.broadcasted_iota(jnp.int32, (c, c), 0)
    col = lax.broadcasted_iota(jnp.int32, (c, c), 1)
    causal = col <= row
    row_k = lax.broadcasted_iota(jnp.int32, (c, dk), 0)

    for hd in range(GLA_HEADS):
        ks = slice(hd * dk, (hd + 1) * dk)
        vs = slice(hd * dv, (hd + 1) * dv)
        q = q_ref[0, :, ks]
        k = k_ref[0, :, ks]
        v = v_ref[0, :, vs]
        bh = b[:, ks]
        mh = mid[:, ks]
        lh = last[:, ks]

        @pl.when(safe)
        def _():
            qd = (q * jnp.exp(bh - mh)).astype(BF16)
            kd = (k * jnp.exp(mh - bh)).astype(BF16)
            sc_ref[...] = _dot_nt(qd, kd)

        @pl.when(jnp.logical_not(safe))
        def _():
            sc_ref[...] = jnp.zeros_like(sc_ref)

            def column(s, carry):
                is_s = row_k == s
                k_s = jnp.sum(jnp.where(is_s, k, 0.0), axis=0, keepdims=True)
                b_s = jnp.sum(jnp.where(is_s, bh, 0.0), axis=0, keepdims=True)
                decay = jnp.exp(jnp.minimum(bh - b_s, 0.0))
                val = jnp.sum(q * k_s * decay, axis=-1, keepdims=True)
                sc_ref[...] += jnp.where(col == s, val, 0.0)
                return carry

            lax.fori_loop(0, c, column, 0)

        scores = jnp.where(causal, sc_ref[...], 0.0).astype(BF16)
        st = state_ref[hd]
        q_in = (q * jnp.exp(bh)).astype(BF16)
        o = _dot(scores, v) + _dot_nt(q_in, st.astype(BF16))
        k_out = (k * jnp.exp(lh - bh)).astype(BF16)
        state_ref[hd] = st * jnp.exp(lh) + _dot_tn(v, k_out)
        on = _rms(o, ng_ref[...])
        r = r_ref[0, :, vs]
        o_ref[0, :, vs] = (on * (r * _sigmoid(r))).astype(o_ref.dtype)


def _gla(q, k, v, r, lg, norm_g, *, c=128):
    b, s, key_w = q.shape
    val_w = v.shape[2]
    dk, dv = key_w // GLA_HEADS, val_w // GLA_HEADS
    tri = (jnp.arange(c)[:, None] >= jnp.arange(c)[None, :]).astype(BF16)
    blk = lambda w: pl.BlockSpec((1, c, w), lambda bi, i: (bi, i, 0))
    return pl.pallas_call(
        _gla_kernel,
        grid=(b, s // c),
        in_specs=[blk(key_w), blk(key_w), blk(val_w), blk(val_w), blk(key_w),
                  _const_spec((c, c)), _const_spec((1, dv))],
        out_specs=blk(val_w),
        out_shape=jax.ShapeDtypeStruct((b, s, val_w), BF16),
        scratch_shapes=[
            pltpu.VMEM((GLA_HEADS, dv, dk), F32),
            pltpu.VMEM((c, c), F32),
        ],
        compiler_params=_params("parallel", "arbitrary"),
        name="gla",
    )(q, k, v, r, lg, tri, norm_g)


def _odd_out_kernel(y_ref, h_ref, w_ref, g_ref, o_ref):
    m = _dot(y_ref[...], w_ref[...])
    o_ref[...] = h_ref[...] + _rms(m, g_ref[...])


def _odd_out(y, h, w_out, g, *, tm=512):
    m, d = h.shape
    n = y.shape[1]
    return pl.pallas_call(
        _odd_out_kernel,
        grid=(m // tm,),
        in_specs=[
            pl.BlockSpec((tm, n), lambda i: (i, 0)),
            pl.BlockSpec((tm, d), lambda i: (i, 0)),
            _const_spec((n, d)),
            _const_spec((1, d)),
        ],
        out_specs=pl.BlockSpec((tm, d), lambda i: (i, 0)),
        out_shape=jax.ShapeDtypeStruct((m, d), F32),
        compiler_params=_params("parallel"),
        name="odd_out",
    )(y, h, w_out, g)


def kernel(x, norm_g, ffn1_w_in, ffn1_w_out, ffn2_w_in, ffn2_w_out, hyb_w_in, hyb_w_out,
           conv_dw_w, conv_dw_b, conv_ln_g, conv_ln_b, gla_w_in, gla_w_gate, gla_b_gate,
           gla_norm_g, gla_w_out):
    b, s, d = x.shape
    m = b * s
    depth = norm_g.shape[0]
    sb_width = SB_HEADS * SB_HEAD_DIM
    conv_ch = conv_dw_w.shape[2]
    key_w = gla_w_gate.shape[2]
    val_w = gla_w_out.shape[1]

    h = x.reshape(m, d)
    for l in range(depth):
        g = norm_g[l][:, None, :]
        h = _ffn(h, g[0], g[1], ffn1_w_in[l].astype(BF16), ffn1_w_out[l].astype(BF16))
        if l % 2 == 0:
            e = l // 2
            qkv, cglu = _even_proj(h, g[2], hyb_w_in[e].astype(BF16), sb_width=sb_width, conv_ch=conv_ch)
            a = _sb_attention(qkv.reshape(b, s, 3 * sb_width), sb_width=sb_width)
            h = _even_out(a, cglu.reshape(b, s, conv_ch), h.reshape(b, s, d), conv_dw_w[e],
                          conv_dw_b[e][None, :], conv_ln_g[e][None, :], conv_ln_b[e][None, :],
                          hyb_w_out[e].astype(BF16), g[3]).reshape(m, d)
        else:
            o = l // 2
            n_main = 2 * key_w + 2 * val_w
            w_in = gla_w_in[o]
            q, k, v, r, lg = _odd_proj(
                h, g[2], w_in[:, :n_main].astype(BF16), w_in[:, n_main:].astype(BF16),
                gla_w_gate[o].astype(BF16), gla_b_gate[o][None, :], key_w=key_w, val_w=val_w)
            to3 = lambda a_: a_.reshape(b, s, a_.shape[-1])
            y = _gla(to3(q), to3(k), to3(v), to3(r), to3(lg), gla_norm_g[o][None, :])
            h = _odd_out(y.reshape(m, val_w), h, gla_w_out[o].astype(BF16), g[3])
        h = _ffn(h, g[4], g[5], ffn2_w_in[l].astype(BF16), ffn2_w_out[l].astype(BF16))
    return h.reshape(b, s, d)
```

```python
import functools

import jax
import jax.numpy as jnp
from jax import lax
from jax.experimental import pallas as pl
from jax.experimental.pallas import tpu as pltpu

F32 = jnp.float32
BF16 = jnp.bfloat16

NORM_EPS = 1e-6
FFN_RES = 0.5
SB_HEADS = 8
SB_HEAD_DIM = 64
CONV_WIDTH = 31
GLA_HEADS = 4
GATE_TAU = 16.0

LANES = 128
SUBLANES = 8
SB_TK = 128
SB_MASKED = -1e30
CONV_HALO = 32
GLA_SAFE_EXP = 80.0
VMEM_LIMIT = 56 * 1024 * 1024


def _params(*sem):
    return pltpu.CompilerParams(dimension_semantics=sem, vmem_limit_bytes=VMEM_LIMIT)


def _const_spec(shape):
    zeros = (0,) * len(shape)
    return pl.BlockSpec(shape, lambda *_: zeros, pipeline_mode=pl.Buffered(1))


def _rms(x, g):
    return x * lax.rsqrt(jnp.mean(x * x, axis=-1, keepdims=True) + NORM_EPS) * g


def _sigmoid(x):
    return 1.0 / (1.0 + jnp.exp(-x))


def _dot(a, b):
    return jnp.dot(a, b, preferred_element_type=F32)


def _dot_nt(a, b):
    return lax.dot_general(a, b, (((1,), (1,)), ((), ())), preferred_element_type=F32)


def _dot_tn(a, b):
    return lax.dot_general(a, b, (((0,), (0,)), ((), ())), preferred_element_type=F32)


def _ffn_kernel(h_ref, gpre_ref, gpost_ref, win_ref, wout_ref, o_ref, *, d_ff, n_chunks):
    h = h_ref[...]
    xn = _rms(h, gpre_ref[...]).astype(BF16)
    tf = d_ff // n_chunks
    f = None
    for c in range(n_chunks):
        a = _dot(xn, win_ref[:, c * tf:(c + 1) * tf])
        b = _dot(xn, win_ref[:, d_ff + c * tf:d_ff + (c + 1) * tf])
        s = (a * _sigmoid(a) * b).astype(BF16)
        part = _dot(s, wout_ref[c * tf:(c + 1) * tf, :])
        f = part if f is None else f + part
    o_ref[...] = h + FFN_RES * _rms(f, gpost_ref[...])


def _ffn(h, g_pre, g_post, w_in, w_out, *, tm=512, n_chunks=11):
    m, d = h.shape
    d_ff = w_out.shape[0]
    return pl.pallas_call(
        functools.partial(_ffn_kernel, d_ff=d_ff, n_chunks=n_chunks),
        grid=(m // tm,),
        in_specs=[
            pl.BlockSpec((tm, d), lambda i: (i, 0)),
            _const_spec((1, d)),
            _const_spec((1, d)),
            _const_spec((d, 2 * d_ff)),
            _const_spec((d_ff, d)),
        ],
        out_specs=pl.BlockSpec((tm, d), lambda i: (i, 0)),
        out_shape=jax.ShapeDtypeStruct((m, d), F32),
        compiler_params=_params("parallel"),
        name="ffn",
    )(h, g_pre, g_post, w_in, w_out)


def _even_proj_kernel(h_ref, g_ref, w_ref, qkv_ref, c_ref, *, sb_width, conv_ch):
    hn = _rms(h_ref[...], g_ref[...]).astype(BF16)
    proj = _dot(hn, w_ref[...])
    scale = SB_HEAD_DIM ** -0.5
    qkv_ref[:, :sb_width] = (proj[:, :sb_width] * scale).astype(BF16)
    qkv_ref[:, sb_width:] = proj[:, sb_width:3 * sb_width].astype(BF16)
    u = proj[:, 3 * sb_width:3 * sb_width + conv_ch]
    gate = proj[:, 3 * sb_width + conv_ch:]
    c_ref[...] = u * _sigmoid(gate)


def _even_proj(h, g, w_in, *, sb_width, conv_ch, tm=512):
    m, d = h.shape
    n = w_in.shape[1]
    return pl.pallas_call(
        functools.partial(_even_proj_kernel, sb_width=sb_width, conv_ch=conv_ch),
        grid=(m // tm,),
        in_specs=[
            pl.BlockSpec((tm, d), lambda i: (i, 0)),
            _const_spec((1, d)),
            _const_spec((d, n)),
        ],
        out_specs=[
            pl.BlockSpec((tm, 3 * sb_width), lambda i: (i, 0)),
            pl.BlockSpec((tm, conv_ch), lambda i: (i, 0)),
        ],
        out_shape=[
            jax.ShapeDtypeStruct((m, 3 * sb_width), BF16),
            jax.ShapeDtypeStruct((m, conv_ch), F32),
        ],
        compiler_params=_params("parallel"),
        name="even_proj",
    )(h, g, w_in)


def _sb_attn_kernel(q_ref, k_ref, v_ref, bd_ref, o_ref, zr, zs, hs, ws, carry, acc):
    tq = q_ref.shape[1]
    tk = SB_TK
    nband = tq // tk
    i = pl.program_id(2)
    q0 = pl.multiple_of(i * tq, tq)
    q = q_ref[0]
    first_head = lax.broadcasted_iota(jnp.int32, (tk, LANES), 1) < SB_HEAD_DIM
    row = lax.broadcasted_iota(jnp.int32, (tq, 2 * tk), 0)
    key_in_block = lax.broadcasted_iota(jnp.int32, (tq, 2 * tk), 1) & (tk - 1)

    def key_start(n):
        return pl.multiple_of(q0 + (nband - 1 - n) * tk, tk)

    def stack_heads(blk):
        zero = jnp.zeros_like(blk)
        return jnp.concatenate([jnp.where(first_head, blk, zero), jnp.where(first_head, zero, blk)], axis=0)

    def score(n, slot, band):
        z = _dot_nt(q, stack_heads(k_ref[0, pl.ds(key_start(n), tk), :]))
        if band:
            z = jnp.where(key_in_block + (nband - 1 - n) * tk < row, z, SB_MASKED)
        zr[slot] = z

    def soft(slot):
        z = zr[slot]
        zs[slot] = z
        zb = z.astype(BF16)
        hs[slot] = jnp.maximum(zb, 0) + jnp.log(1 + jnp.exp(-jnp.abs(zb)))

    def weight(slot):
        bd = bd_ref[...]
        cum = _dot(hs[slot], bd)
        c = carry[...]
        ws[slot] = jnp.exp(zs[slot] + cum + c).astype(BF16)
        total = jnp.concatenate([jnp.broadcast_to(cum[:, 0:1], (tq, tk)),
                                 jnp.broadcast_to(cum[:, tk:tk + 1], (tq, tk))], axis=1)
        carry[...] = c + total

    def output(n, slot):
        acc[...] += _dot(ws[slot], stack_heads(v_ref[0, pl.ds(key_start(n), tk), :]))

    def step(n, slot, band):
        if not band or n >= 3:
            output(n - 3, 1 - slot)
        if not band or n >= 2:
            weight(slot)
        if not band or n >= 1:
            soft(1 - slot)
        score(n, slot, band)

    carry[...] = jnp.zeros_like(carry)
    acc[...] = jnp.zeros_like(acc)
    for n in range(nband):
        step(n, n % 2, True)

    def two_steps(m, _):
        n = nband + 2 * m
        step(n, 0, False)
        step(n + 1, 1, False)
        return 0

    lax.fori_loop(0, i * (nband // 2), two_steps, 0)
    last = (i + 1) * nband
    output(last - 3, 1)
    weight(0)
    soft(1)
    output(last - 2, 0)
    weight(1)
    output(last - 1, 1)
    o_ref[0] = acc[...].astype(o_ref.dtype)


def _sb_attention(qkv, *, sb_width, tq=512):
    b, s, _ = qkv.shape
    tk = SB_TK
    assert tq % (2 * tk) == 0 and s % tq == 0
    pairs = sb_width // LANES
    tri = (jnp.arange(tk)[:, None] >= jnp.arange(tk)[None, :]).astype(F32)
    zero = jnp.zeros_like(tri)
    bd = -jnp.block([[tri, zero], [zero, tri]]).astype(BF16)
    return pl.pallas_call(
        _sb_attn_kernel,
        grid=(b, pairs, s // tq),
        in_specs=[
            pl.BlockSpec((1, tq, LANES), lambda bi, p, i: (bi, i, p)),
            pl.BlockSpec((1, s, LANES), lambda bi, p, i: (bi, 0, pairs + p)),
            pl.BlockSpec((1, s, LANES), lambda bi, p, i: (bi, 0, 2 * pairs + p)),
            _const_spec((2 * tk, 2 * tk)),
        ],
        out_specs=pl.BlockSpec((1, tq, LANES), lambda bi, p, i: (bi, i, p)),
        out_shape=jax.ShapeDtypeStruct((b, s, sb_width), BF16),
        scratch_shapes=[
            pltpu.VMEM((2, tq, 2 * tk), F32),
            pltpu.VMEM((2, tq, 2 * tk), F32),
            pltpu.VMEM((2, tq, 2 * tk), BF16),
            pltpu.VMEM((2, tq, 2 * tk), BF16),
            pltpu.VMEM((tq, 2 * tk), F32),
            pltpu.VMEM((tq, LANES), F32),
        ],
        compiler_params=_params("parallel", "parallel", "arbitrary"),
        name="sb_attention",
    )(qkv, qkv, qkv, bd)


def _even_out_kernel(a_ref, c_ref, halo_ref, h_ref, dww_ref, dwb_ref, lng_ref, lnb_ref,
                     wout_ref, g_ref, o_ref, xbuf, xshift, ybuf, *, rows):
    t = c_ref.shape[1]
    ch = c_ref.shape[2]
    i = pl.program_id(1)
    halo = halo_ref[0]
    xbuf[0:CONV_HALO, :] = jnp.where(i > 0, halo, jnp.zeros_like(halo))
    xbuf[CONV_HALO:, :] = c_ref[0]
    span = xshift.shape[1]
    for r in range(1, SUBLANES):
        xshift[r - 1] = xbuf[r:r + span, :]
    first = CONV_HALO - (CONV_WIDTH - 1)
    for r0 in range(0, t, rows):
        for c0 in range(0, ch, LANES):
            acc = jnp.zeros((rows, LANES), F32) + dwb_ref[:, c0:c0 + LANES]
            for k in range(CONV_WIDTH):
                r = (first + k) % SUBLANES
                base = first + k - r + r0
                if r == 0:
                    x = xbuf[base:base + rows, c0:c0 + LANES]
                else:
                    x = xshift[r - 1, base:base + rows, c0:c0 + LANES]
                acc = acc + dww_ref[k:k + 1, c0:c0 + LANES] * x
            ybuf[r0:r0 + rows, c0:c0 + LANES] = acc
    y = ybuf[...]
    mu = jnp.mean(y, axis=-1, keepdims=True)
    yc = y - mu
    var = jnp.mean(yc * yc, axis=-1, keepdims=True)
    yn = yc * lax.rsqrt(var + NORM_EPS) * lng_ref[...] + lnb_ref[...]
    cact = (yn * _sigmoid(yn)).astype(BF16)
    aw = a_ref.shape[2]
    m = _dot(a_ref[0], wout_ref[:aw, :]) + _dot(cact, wout_ref[aw:, :])
    o_ref[0] = h_ref[0] + _rms(m, g_ref[...])


def _even_out(a, c, h, dw_w, dw_b, ln_g, ln_b, w_out, g, *, t=256, rows=64):
    b, s, d = h.shape
    aw, ch = a.shape[2], c.shape[2]
    per = t // CONV_HALO
    return pl.pallas_call(
        functools.partial(_even_out_kernel, rows=rows),
        grid=(b, s // t),
        in_specs=[
            pl.BlockSpec((1, t, aw), lambda bi, i: (bi, i, 0)),
            pl.BlockSpec((1, t, ch), lambda bi, i: (bi, i, 0)),
            pl.BlockSpec((1, CONV_HALO, ch), lambda bi, i: (bi, jnp.maximum(i * per - 1, 0), 0)),
            pl.BlockSpec((1, t, d), lambda bi, i: (bi, i, 0)),
            _const_spec((CONV_WIDTH, ch)),
            _const_spec((1, ch)),
            _const_spec((1, ch)),
            _const_spec((1, ch)),
            _const_spec((aw + ch, d)),
            _const_spec((1, d)),
        ],
        out_specs=pl.BlockSpec((1, t, d), lambda bi, i: (bi, i, 0)),
        out_shape=jax.ShapeDtypeStruct((b, s, d), F32),
        scratch_shapes=[
            pltpu.VMEM((t + CONV_HALO, ch), F32),
            pltpu.VMEM((SUBLANES - 1, t + CONV_HALO - SUBLANES, ch), F32),
            pltpu.VMEM((t, ch), F32),
        ],
        compiler_params=_params("parallel", "parallel"),
        name="even_out",
    )(a, c, c, h, dw_w, dw_b, ln_g, ln_b, w_out, g)


def _odd_proj_kernel(h_ref, g_ref, w_ref, wlr_ref, wgate_ref, bgate_ref,
                     q_ref, k_ref, v_ref, r_ref, lg_ref, *, key_w, val_w):
    hn = _rms(h_ref[...], g_ref[...]).astype(BF16)
    proj = _dot(hn, w_ref[...])
    dk = key_w // GLA_HEADS
    q_ref[...] = proj[:, :key_w] * (dk ** -0.5)
    k_ref[...] = proj[:, key_w:2 * key_w]
    v_ref[...] = proj[:, 2 * key_w:2 * key_w + val_w].astype(BF16)
    r_ref[...] = proj[:, 2 * key_w + val_w:]
    g_lr = _dot(hn, wlr_ref[...]).astype(BF16)
    x = _dot(g_lr, wgate_ref[...]) + bgate_ref[...]
    lg_ref[...] = -(jnp.maximum(-x, 0.0) + jnp.log(1.0 + jnp.exp(-jnp.abs(x)))) / GATE_TAU


def _odd_proj(h, g, w_main, w_lr, w_gate, b_gate, *, key_w, val_w, tm=512):
    m, d = h.shape
    rank = w_lr.shape[1]
    row = lambda i: (i, 0)
    return pl.pallas_call(
        functools.partial(_odd_proj_kernel, key_w=key_w, val_w=val_w),
        grid=(m // tm,),
        in_specs=[
            pl.BlockSpec((tm, d), row),
            _const_spec((1, d)),
            _const_spec((d, 2 * key_w + 2 * val_w)),
            _const_spec((d, rank)),
            _const_spec((rank, key_w)),
            _const_spec((1, key_w)),
        ],
        out_specs=[
            pl.BlockSpec((tm, key_w), row),
            pl.BlockSpec((tm, key_w), row),
            pl.BlockSpec((tm, val_w), row),
            pl.BlockSpec((tm, val_w), row),
            pl.BlockSpec((tm, key_w), row),
        ],
        out_shape=[
            jax.ShapeDtypeStruct((m, key_w), F32),
            jax.ShapeDtypeStruct((m, key_w), F32),
            jax.ShapeDtypeStruct((m, val_w), BF16),
            jax.ShapeDtypeStruct((m, val_w), F32),
            jax.ShapeDtypeStruct((m, key_w), F32),
        ],
        compiler_params=_params("parallel"),
        name="odd_proj",
    )(h, g, w_main, w_lr, w_gate, b_gate)


def _gla_kernel(q_ref, k_ref, v_ref, r_ref, lg_ref, tri_ref, ng_ref, o_ref,
                state_ref, sc_ref):
    c = q_ref.shape[1]
    key_w = q_ref.shape[2]
    dk = key_w // GLA_HEADS
    dv = v_ref.shape[2] // GLA_HEADS

    @pl.when(pl.program_id(1) == 0)
    def _():
        state_ref[...] = jnp.zeros_like(state_ref)

    lg = lg_ref[0]
    g1 = lg.astype(BF16)
    rem = lg - g1.astype(F32)
    g2 = rem.astype(BF16)
    g3 = (rem - g2.astype(F32)).astype(BF16)
    tri = tri_ref[...]
    b = _dot(tri, g1) + _dot(tri, g2) + _dot(tri, g3)
    mid = b[c // 2:c // 2 + 1, :]
    last = b[c - 1:c, :]
    safe = jnp.max(jnp.abs(b - mid)) < GLA_SAFE_EXP

    row = lax.broadcasted_iota(jnp.int32, (c, c), 0)
    col = lax.broadcasted_iota(jnp.int32, (c, c), 1)
    causal = col <= row
    row_k = lax.broadcasted_iota(jnp.int32, (c, dk), 0)

    def rebuilt_scores(q, k, bh):
        sc_ref[...] = jnp.zeros_like(sc_ref)

        def column(s, carry):
            is_s = row_k == s
            k_s = jnp.sum(jnp.where(is_s, k, 0.0), axis=0, keepdims=True)
            b_s = jnp.sum(jnp.where(is_s, bh, 0.0), axis=0, keepdims=True)
            decay = jnp.exp(jnp.minimum(bh - b_s, 0.0))
            val = jnp.sum(q * k_s * decay, axis=-1, keepdims=True)
            sc_ref[...] += jnp.where(col == s, val, 0.0)
            return carry

        lax.fori_loop(0, c, column, 0)
        return sc_ref[...]

    def all_heads(factored):
        for hd in range(GLA_HEADS):
            ks = slice(hd * dk, (hd + 1) * dk)
            vs = slice(hd * dv, (hd + 1) * dv)
            q = q_ref[0, :, ks]
            k = k_ref[0, :, ks]
            v = v_ref[0, :, vs]
            bh = b[:, ks]
            mh = mid[:, ks]
            lh = last[:, ks]
            if factored:
                qd = (q * jnp.exp(bh - mh)).astype(BF16)
                kd = (k * jnp.exp(mh - bh)).astype(BF16)
                raw = _dot_nt(qd, kd)
            else:
                raw = rebuilt_scores(q, k, bh)
            scores = jnp.where(causal, raw, 0.0).astype(BF16)
            st = state_ref[hd]
            q_in = (q * jnp.exp(bh)).astype(BF16)
            o = _dot(scores, v) + _dot_nt(q_in, st.astype(BF16))
            k_out = (k * jnp.exp(lh - bh)).astype(BF16)
            state_ref[hd] = st * jnp.exp(lh) + _dot_tn(v, k_out)
            on = _rms(o, ng_ref[...])
            r = r_ref[0, :, vs]
            o_ref[0, :, vs] = (on * (r * _sigmoid(r))).astype(o_ref.dtype)

    pl.when(safe)(lambda: all_heads(True))
    pl.when(jnp.logical_not(safe))(lambda: all_heads(False))


def _gla(q, k, v, r, lg, norm_g, *, c=128):
    b, s, key_w = q.shape
    val_w = v.shape[2]
    dk, dv = key_w // GLA_HEADS, val_w // GLA_HEADS
    tri = (jnp.arange(c)[:, None] >= jnp.arange(c)[None, :]).astype(BF16)
    blk = lambda w: pl.BlockSpec((1, c, w), lambda bi, i: (bi, i, 0))
    return pl.pallas_call(
        _gla_kernel,
        grid=(b, s // c),
        in_specs=[blk(key_w), blk(key_w), blk(val_w), blk(val_w), blk(key_w),
                  _const_spec((c, c)), _const_spec((1, dv))],
        out_specs=blk(val_w),
        out_shape=jax.ShapeDtypeStruct((b, s, val_w), BF16),
        scratch_shapes=[
            pltpu.VMEM((GLA_HEADS, dv, dk), F32),
            pltpu.VMEM((c, c), F32),
        ],
        compiler_params=_params("parallel", "arbitrary"),
        name="gla",
    )(q, k, v, r, lg, tri, norm_g)


def _odd_out_kernel(y_ref, h_ref, w_ref, g_ref, o_ref):
    m = _dot(y_ref[...], w_ref[...])
    o_ref[...] = h_ref[...] + _rms(m, g_ref[...])


def _odd_out(y, h, w_out, g, *, tm=512):
    m, d = h.shape
    n = y.shape[1]
    return pl.pallas_call(
        _odd_out_kernel,
        grid=(m // tm,),
        in_specs=[
            pl.BlockSpec((tm, n), lambda i: (i, 0)),
            pl.BlockSpec((tm, d), lambda i: (i, 0)),
            _const_spec((n, d)),
            _const_spec((1, d)),
        ],
        out_specs=pl.BlockSpec((tm, d), lambda i: (i, 0)),
        out_shape=jax.ShapeDtypeStruct((m, d), F32),
        compiler_params=_params("parallel"),
        name="odd_out",
    )(y, h, w_out, g)


def kernel(x, norm_g, ffn1_w_in, ffn1_w_out, ffn2_w_in, ffn2_w_out, hyb_w_in, hyb_w_out,
           conv_dw_w, conv_dw_b, conv_ln_g, conv_ln_b, gla_w_in, gla_w_gate, gla_b_gate,
           gla_norm_g, gla_w_out):
    b, s, d = x.shape
    m = b * s
    depth = norm_g.shape[0]
    sb_width = SB_HEADS * SB_HEAD_DIM
    conv_ch = conv_dw_w.shape[2]
    key_w = gla_w_gate.shape[2]
    val_w = gla_w_out.shape[1]

    h = x.reshape(m, d)
    for l in range(depth):
        g = norm_g[l][:, None, :]
        h = _ffn(h, g[0], g[1], ffn1_w_in[l].astype(BF16), ffn1_w_out[l].astype(BF16))
        if l % 2 == 0:
            e = l // 2
            qkv, cglu = _even_proj(h, g[2], hyb_w_in[e].astype(BF16), sb_width=sb_width, conv_ch=conv_ch)
            a = _sb_attention(qkv.reshape(b, s, 3 * sb_width), sb_width=sb_width)
            h = _even_out(a, cglu.reshape(b, s, conv_ch), h.reshape(b, s, d), conv_dw_w[e],
                          conv_dw_b[e][None, :], conv_ln_g[e][None, :], conv_ln_b[e][None, :],
                          hyb_w_out[e].astype(BF16), g[3]).reshape(m, d)
        else:
            o = l // 2
            n_main = 2 * key_w + 2 * val_w
            w_in = gla_w_in[o]
            q, k, v, r, lg = _odd_proj(
                h, g[2], w_in[:, :n_main].astype(BF16), w_in[:, n_main:].astype(BF16),
                gla_w_gate[o].astype(BF16), gla_b_gate[o][None, :], key_w=key_w, val_w=val_w)
            to3 = lambda a_: a_.reshape(b, s, a_.shape[-1])
            y = _gla(to3(q), to3(k), to3(v), to3(r), to3(lg), gla_norm_g[o][None, :])
            h = _odd_out(y.reshape(m, val_w), h, gla_w_out[o].astype(BF16), g[3])
        h = _ffn(h, g[4], g[5], ffn2_w_in[l].astype(BF16), ffn2_w_out[l].astype(BF16))
    return h.reshape(b, s, d)
```

```python
import functools

import jax
import jax.numpy as jnp
from jax import lax
from jax.experimental import pallas as pl
from jax.experimental.pallas import tpu as pltpu

F32 = jnp.float32
BF16 = jnp.bfloat16

NORM_EPS = 1e-6
FFN_RES = 0.5
SB_HEADS = 8
SB_HEAD_DIM = 64
CONV_WIDTH = 31
GLA_HEADS = 4
GATE_TAU = 16.0

LANES = 128
SUBLANES = 8
SB_TK = 128
SB_MASKED = -1e30
CONV_HALO = 32
GLA_SAFE_EXP = 80.0
VMEM_LIMIT = 56 * 1024 * 1024


def _params(*sem):
    return pltpu.CompilerParams(dimension_semantics=sem, vmem_limit_bytes=VMEM_LIMIT)


def _const_spec(shape):
    zeros = (0,) * len(shape)
    return pl.BlockSpec(shape, lambda *_: zeros, pipeline_mode=pl.Buffered(1))


def _rms(x, g):
    return x * lax.rsqrt(jnp.mean(x * x, axis=-1, keepdims=True) + NORM_EPS) * g


def _sigmoid(x):
    return 1.0 / (1.0 + jnp.exp(-x))


def _dot(a, b):
    return jnp.dot(a, b, preferred_element_type=F32)


def _dot_nt(a, b):
    return lax.dot_general(a, b, (((1,), (1,)), ((), ())), preferred_element_type=F32)


def _dot_tn(a, b):
    return lax.dot_general(a, b, (((0,), (0,)), ((), ())), preferred_element_type=F32)


def _ffn_kernel(h_ref, gpre_ref, gpost_ref, win_ref, wout_ref, o_ref, *, d_ff, n_chunks):
    h = h_ref[...]
    xn = _rms(h, gpre_ref[...]).astype(BF16)
    tf = d_ff // n_chunks
    f = None
    for c in range(n_chunks):
        a = _dot(xn, win_ref[:, c * tf:(c + 1) * tf])
        b = _dot(xn, win_ref[:, d_ff + c * tf:d_ff + (c + 1) * tf])
        s = (a * _sigmoid(a) * b).astype(BF16)
        part = _dot(s, wout_ref[c * tf:(c + 1) * tf, :])
        f = part if f is None else f + part
    o_ref[...] = h + FFN_RES * _rms(f, gpost_ref[...])


def _ffn(h, g_pre, g_post, w_in, w_out, *, tm=512, n_chunks=11):
    m, d = h.shape
    d_ff = w_out.shape[0]
    return pl.pallas_call(
        functools.partial(_ffn_kernel, d_ff=d_ff, n_chunks=n_chunks),
        grid=(m // tm,),
        in_specs=[
            pl.BlockSpec((tm, d), lambda i: (i, 0)),
            _const_spec((1, d)),
            _const_spec((1, d)),
            _const_spec((d, 2 * d_ff)),
            _const_spec((d_ff, d)),
        ],
        out_specs=pl.BlockSpec((tm, d), lambda i: (i, 0)),
        out_shape=jax.ShapeDtypeStruct((m, d), F32),
        compiler_params=_params("parallel"),
        name="ffn",
    )(h, g_pre, g_post, w_in, w_out)


def _even_proj_kernel(h_ref, g_ref, w_ref, qkv_ref, c_ref, *, sb_width, conv_ch):
    hn = _rms(h_ref[...], g_ref[...]).astype(BF16)
    proj = _dot(hn, w_ref[...])
    scale = SB_HEAD_DIM ** -0.5
    qkv_ref[:, :sb_width] = (proj[:, :sb_width] * scale).astype(BF16)
    qkv_ref[:, sb_width:] = proj[:, sb_width:3 * sb_width].astype(BF16)
    u = proj[:, 3 * sb_width:3 * sb_width + conv_ch]
    gate = proj[:, 3 * sb_width + conv_ch:]
    c_ref[...] = u * _sigmoid(gate)


def _even_proj(h, g, w_in, *, sb_width, conv_ch, tm=512):
    m, d = h.shape
    n = w_in.shape[1]
    return pl.pallas_call(
        functools.partial(_even_proj_kernel, sb_width=sb_width, conv_ch=conv_ch),
        grid=(m // tm,),
        in_specs=[
            pl.BlockSpec((tm, d), lambda i: (i, 0)),
            _const_spec((1, d)),
            _const_spec((d, n)),
        ],
        out_specs=[
            pl.BlockSpec((tm, 3 * sb_width), lambda i: (i, 0)),
            pl.BlockSpec((tm, conv_ch), lambda i: (i, 0)),
        ],
        out_shape=[
            jax.ShapeDtypeStruct((m, 3 * sb_width), BF16),
            jax.ShapeDtypeStruct((m, conv_ch), F32),
        ],
        compiler_params=_params("parallel"),
        name="even_proj",
    )(h, g, w_in)


def _sb_attn_kernel(q_ref, k_ref, v_ref, bd_ref, o_ref, zr, zs, hs, ws, carry, acc):
    tq = q_ref.shape[1]
    tk = SB_TK
    nband = tq // tk
    i = pl.program_id(2)
    q0 = pl.multiple_of(i * tq, tq)
    q = q_ref[0]
    first_head = lax.broadcasted_iota(jnp.int32, (tk, LANES), 1) < SB_HEAD_DIM
    row = lax.broadcasted_iota(jnp.int32, (tq, 2 * tk), 0)
    key_in_block = lax.broadcasted_iota(jnp.int32, (tq, 2 * tk), 1) & (tk - 1)

    def key_start(n):
        return pl.multiple_of(q0 + (nband - 1 - n) * tk, tk)

    def stack_heads(blk):
        zero = jnp.zeros_like(blk)
        return jnp.concatenate([jnp.where(first_head, blk, zero), jnp.where(first_head, zero, blk)], axis=0)

    def score(n, slot, band):
        z = _dot_nt(q, stack_heads(k_ref[0, pl.ds(key_start(n), tk), :]))
        if band:
            z = jnp.where(key_in_block + (nband - 1 - n) * tk < row, z, SB_MASKED)
        zr[slot] = z

    def soft(slot):
        z = zr[slot]
        zs[slot] = z
        zb = z.astype(BF16)
        hs[slot] = jnp.maximum(zb, 0) + jnp.log(1 + jnp.exp(-jnp.abs(zb)))

    def weight(slot):
        bd = bd_ref[...]
        cum = _dot(hs[slot], bd)
        c = carry[...]
        ws[slot] = jnp.exp(zs[slot] + cum + c).astype(BF16)
        total = jnp.concatenate([jnp.broadcast_to(cum[:, 0:1], (tq, tk)),
                                 jnp.broadcast_to(cum[:, tk:tk + 1], (tq, tk))], axis=1)
        carry[...] = c + total

    def output(n, slot):
        acc[...] += _dot(ws[slot], stack_heads(v_ref[0, pl.ds(key_start(n), tk), :]))

    def step(n, slot, band):
        if not band or n >= 2:
            weight(slot)
        if not band or n >= 1:
            soft(1 - slot)
        if not band or n >= 3:
            output(n - 3, 1 - slot)
        score(n, slot, band)

    carry[...] = jnp.zeros_like(carry)
    acc[...] = jnp.zeros_like(acc)
    for n in range(nband):
        step(n, n % 2, True)

    def band_of_steps(m, _):
        for j in range(nband):
            step(nband * (m + 1) + j, j % 2, False)
        return 0

    lax.fori_loop(0, i, band_of_steps, 0)
    last = (i + 1) * nband
    output(last - 3, 1)
    weight(0)
    soft(1)
    output(last - 2, 0)
    weight(1)
    output(last - 1, 1)
    o_ref[0] = acc[...].astype(o_ref.dtype)


def _sb_attention(qkv, *, sb_width, tq=512):
    b, s, _ = qkv.shape
    tk = SB_TK
    assert tq % (2 * tk) == 0 and s % tq == 0
    pairs = sb_width // LANES
    tri = (jnp.arange(tk)[:, None] >= jnp.arange(tk)[None, :]).astype(F32)
    zero = jnp.zeros_like(tri)
    bd = -jnp.block([[tri, zero], [zero, tri]]).astype(BF16)
    return pl.pallas_call(
        _sb_attn_kernel,
        grid=(b, pairs, s // tq),
        in_specs=[
            pl.BlockSpec((1, tq, LANES), lambda bi, p, i: (bi, i, p)),
            pl.BlockSpec((1, s, LANES), lambda bi, p, i: (bi, 0, pairs + p)),
            pl.BlockSpec((1, s, LANES), lambda bi, p, i: (bi, 0, 2 * pairs + p)),
            _const_spec((2 * tk, 2 * tk)),
        ],
        out_specs=pl.BlockSpec((1, tq, LANES), lambda bi, p, i: (bi, i, p)),
        out_shape=jax.ShapeDtypeStruct((b, s, sb_width), BF16),
        scratch_shapes=[
            pltpu.VMEM((2, tq, 2 * tk), F32),
            pltpu.VMEM((2, tq, 2 * tk), F32),
            pltpu.VMEM((2, tq, 2 * tk), BF16),
            pltpu.VMEM((2, tq, 2 * tk), BF16),
            pltpu.VMEM((tq, 2 * tk), F32),
            pltpu.VMEM((tq, LANES), F32),
        ],
        compiler_params=_params("parallel", "parallel", "arbitrary"),
        name="sb_attention",
    )(qkv, qkv, qkv, bd)


def _even_out_kernel(a_ref, c_ref, halo_ref, h_ref, dww_ref, dwb_ref, lng_ref, lnb_ref,
                     wout_ref, g_ref, o_ref, xbuf, xshift, ybuf, *, rows):
    t = c_ref.shape[1]
    ch = c_ref.shape[2]
    i = pl.program_id(1)
    halo = halo_ref[0]
    xbuf[0:CONV_HALO, :] = jnp.where(i > 0, halo, jnp.zeros_like(halo))
    xbuf[CONV_HALO:, :] = c_ref[0]
    span = xshift.shape[1]
    for r in range(1, SUBLANES):
        xshift[r - 1] = xbuf[r:r + span, :]
    first = CONV_HALO - (CONV_WIDTH - 1)
    for r0 in range(0, t, rows):
        for c0 in range(0, ch, LANES):
            acc = jnp.zeros((rows, LANES), F32) + dwb_ref[:, c0:c0 + LANES]
            for k in range(CONV_WIDTH):
                r = (first + k) % SUBLANES
                base = first + k - r + r0
                if r == 0:
                    x = xbuf[base:base + rows, c0:c0 + LANES]
                else:
                    x = xshift[r - 1, base:base + rows, c0:c0 + LANES]
                acc = acc + dww_ref[k:k + 1, c0:c0 + LANES] * x
            ybuf[r0:r0 + rows, c0:c0 + LANES] = acc
    y = ybuf[...]
    mu = jnp.mean(y, axis=-1, keepdims=True)
    yc = y - mu
    var = jnp.mean(yc * yc, axis=-1, keepdims=True)
    yn = yc * lax.rsqrt(var + NORM_EPS) * lng_ref[...] + lnb_ref[...]
    cact = (yn * _sigmoid(yn)).astype(BF16)
    aw = a_ref.shape[2]
    m = _dot(a_ref[0], wout_ref[:aw, :]) + _dot(cact, wout_ref[aw:, :])
    o_ref[0] = h_ref[0] + _rms(m, g_ref[...])


def _even_out(a, c, h, dw_w, dw_b, ln_g, ln_b, w_out, g, *, t=256, rows=64):
    b, s, d = h.shape
    aw, ch = a.shape[2], c.shape[2]
    per = t // CONV_HALO
    return pl.pallas_call(
        functools.partial(_even_out_kernel, rows=rows),
        grid=(b, s // t),
        in_specs=[
            pl.BlockSpec((1, t, aw), lambda bi, i: (bi, i, 0)),
            pl.BlockSpec((1, t, ch), lambda bi, i: (bi, i, 0)),
            pl.BlockSpec((1, CONV_HALO, ch), lambda bi, i: (bi, jnp.maximum(i * per - 1, 0), 0)),
            pl.BlockSpec((1, t, d), lambda bi, i: (bi, i, 0)),
            _const_spec((CONV_WIDTH, ch)),
            _const_spec((1, ch)),
            _const_spec((1, ch)),
            _const_spec((1, ch)),
            _const_spec((aw + ch, d)),
            _const_spec((1, d)),
        ],
        out_specs=pl.BlockSpec((1, t, d), lambda bi, i: (bi, i, 0)),
        out_shape=jax.ShapeDtypeStruct((b, s, d), F32),
        scratch_shapes=[
            pltpu.VMEM((t + CONV_HALO, ch), F32),
            pltpu.VMEM((SUBLANES - 1, t + CONV_HALO - SUBLANES, ch), F32),
            pltpu.VMEM((t, ch), F32),
        ],
        compiler_params=_params("parallel", "parallel"),
        name="even_out",
    )(a, c, c, h, dw_w, dw_b, ln_g, ln_b, w_out, g)


def _odd_proj_kernel(h_ref, g_ref, w_ref, wgate_ref, bgate_ref,
                     q_ref, k_ref, v_ref, r_ref, lg_ref, *, key_w, val_w):
    hn = _rms(h_ref[...], g_ref[...]).astype(BF16)
    proj = _dot(hn, w_ref[...])
    dk = key_w // GLA_HEADS
    q_ref[...] = proj[:, :key_w] * (dk ** -0.5)
    k_ref[...] = proj[:, key_w:2 * key_w]
    v_ref[...] = proj[:, 2 * key_w:2 * key_w + val_w].astype(BF16)
    r_ref[...] = proj[:, 2 * key_w + val_w:2 * key_w + 2 * val_w]
    g_lr = proj[:, 2 * key_w + 2 * val_w:].astype(BF16)
    x = _dot(g_lr, wgate_ref[...]) + bgate_ref[...]
    lg_ref[...] = -(jnp.maximum(-x, 0.0) + jnp.log(1.0 + jnp.exp(-jnp.abs(x)))) / GATE_TAU


def _odd_proj(h, g, w_in, w_gate, b_gate, *, key_w, val_w, tm=512):
    m, d = h.shape
    rank = w_gate.shape[0]
    row = lambda i: (i, 0)
    return pl.pallas_call(
        functools.partial(_odd_proj_kernel, key_w=key_w, val_w=val_w),
        grid=(m // tm,),
        in_specs=[
            pl.BlockSpec((tm, d), row),
            _const_spec((1, d)),
            _const_spec((d, 2 * key_w + 2 * val_w + rank)),
            _const_spec((rank, key_w)),
            _const_spec((1, key_w)),
        ],
        out_specs=[
            pl.BlockSpec((tm, key_w), row),
            pl.BlockSpec((tm, key_w), row),
            pl.BlockSpec((tm, val_w), row),
            pl.BlockSpec((tm, val_w), row),
            pl.BlockSpec((tm, key_w), row),
        ],
        out_shape=[
            jax.ShapeDtypeStruct((m, key_w), F32),
            jax.ShapeDtypeStruct((m, key_w), F32),
            jax.ShapeDtypeStruct((m, val_w), BF16),
            jax.ShapeDtypeStruct((m, val_w), F32),
            jax.ShapeDtypeStruct((m, key_w), F32),
        ],
        compiler_params=_params("parallel"),
        name="odd_proj",
    )(h, g, w_in, w_gate, b_gate)


def _gla_kernel(q_ref, k_ref, v_ref, r_ref, lg_ref, tri_ref, ng_ref, o_ref,
                state_ref, sc_ref):
    nb, c, key_w = q_ref.shape
    dk = key_w // GLA_HEADS
    dv = v_ref.shape[2] // GLA_HEADS

    @pl.when(pl.program_id(0) == 0)
    def _():
        state_ref[...] = jnp.zeros_like(state_ref)

    tri = tri_ref[...]
    bs, mids, lasts = [], [], []
    spread = None
    for bi in range(nb):
        lg = lg_ref[bi]
        g1 = lg.astype(BF16)
        rem = lg - g1.astype(F32)
        g2 = rem.astype(BF16)
        g3 = (rem - g2.astype(F32)).astype(BF16)
        b = _dot(tri, g1) + _dot(tri, g2) + _dot(tri, g3)
        mid = b[c // 2:c // 2 + 1, :]
        bs.append(b)
        mids.append(mid)
        lasts.append(b[c - 1:c, :])
        dev = jnp.abs(b - mid)
        spread = dev if spread is None else jnp.maximum(spread, dev)
    safe = jnp.max(spread) < GLA_SAFE_EXP

    row = lax.broadcasted_iota(jnp.int32, (c, c), 0)
    col = lax.broadcasted_iota(jnp.int32, (c, c), 1)
    causal = col <= row
    row_k = lax.broadcasted_iota(jnp.int32, (c, dk), 0)

    def rebuilt_scores(q, k, bh):
        sc_ref[...] = jnp.zeros_like(sc_ref)

        def column(s, carry):
            is_s = row_k == s
            k_s = jnp.sum(jnp.where(is_s, k, 0.0), axis=0, keepdims=True)
            b_s = jnp.sum(jnp.where(is_s, bh, 0.0), axis=0, keepdims=True)
            decay = jnp.exp(jnp.minimum(bh - b_s, 0.0))
            val = jnp.sum(q * k_s * decay, axis=-1, keepdims=True)
            sc_ref[...] += jnp.where(col == s, val, 0.0)
            return carry

        lax.fori_loop(0, c, column, 0)
        return sc_ref[...]

    def all_heads(factored):
        for bi in range(nb):
            for hd in range(GLA_HEADS):
                ks = slice(hd * dk, (hd + 1) * dk)
                vs = slice(hd * dv, (hd + 1) * dv)
                q = q_ref[bi, :, ks]
                k = k_ref[bi, :, ks]
                v = v_ref[bi, :, vs]
                bh = bs[bi][:, ks]
                mh = mids[bi][:, ks]
                lh = lasts[bi][:, ks]
                if factored:
                    qd = (q * jnp.exp(bh - mh)).astype(BF16)
                    kd = (k * jnp.exp(mh - bh)).astype(BF16)
                    raw = _dot_nt(qd, kd)
                else:
                    raw = rebuilt_scores(q, k, bh)
                scores = jnp.where(causal, raw, 0.0).astype(BF16)
                st = state_ref[bi, hd]
                q_in = (q * jnp.exp(bh)).astype(BF16)
                o = _dot(scores, v) + _dot_nt(q_in, st.astype(BF16))
                k_out = (k * jnp.exp(lh - bh)).astype(BF16)
                state_ref[bi, hd] = st * jnp.exp(lh) + _dot_tn(v, k_out)
                on = _rms(o, ng_ref[...])
                r = r_ref[bi, :, vs]
                o_ref[bi, :, vs] = (on * (r * _sigmoid(r))).astype(o_ref.dtype)

    pl.when(safe)(lambda: all_heads(True))
    pl.when(jnp.logical_not(safe))(lambda: all_heads(False))


def _gla(q, k, v, r, lg, norm_g, *, c=128):
    b, s, key_w = q.shape
    val_w = v.shape[2]
    dk, dv = key_w // GLA_HEADS, val_w // GLA_HEADS
    tri = (jnp.arange(c)[:, None] >= jnp.arange(c)[None, :]).astype(BF16)
    blk = lambda w: pl.BlockSpec((b, c, w), lambda i: (0, i, 0))
    return pl.pallas_call(
        _gla_kernel,
        grid=(s // c,),
        in_specs=[blk(key_w), blk(key_w), blk(val_w), blk(val_w), blk(key_w),
                  _const_spec((c, c)), _const_spec((1, dv))],
        out_specs=blk(val_w),
        out_shape=jax.ShapeDtypeStruct((b, s, val_w), BF16),
        scratch_shapes=[
            pltpu.VMEM((b, GLA_HEADS, dv, dk), F32),
            pltpu.VMEM((c, c), F32),
        ],
        compiler_params=_params("arbitrary"),
        name="gla",
    )(q, k, v, r, lg, tri, norm_g)


def _odd_out_kernel(y_ref, h_ref, w_ref, g_ref, o_ref):
    m = _dot(y_ref[...], w_ref[...])
    o_ref[...] = h_ref[...] + _rms(m, g_ref[...])


def _odd_out(y, h, w_out, g, *, tm=512):
    m, d = h.shape
    n = y.shape[1]
    return pl.pallas_call(
        _odd_out_kernel,
        grid=(m // tm,),
        in_specs=[
            pl.BlockSpec((tm, n), lambda i: (i, 0)),
            pl.BlockSpec((tm, d), lambda i: (i, 0)),
            _const_spec((n, d)),
            _const_spec((1, d)),
        ],
        out_specs=pl.BlockSpec((tm, d), lambda i: (i, 0)),
        out_shape=jax.ShapeDtypeStruct((m, d), F32),
        compiler_params=_params("parallel"),
        name="odd_out",
    )(y, h, w_out, g)


def kernel(x, norm_g, ffn1_w_in, ffn1_w_out, ffn2_w_in, ffn2_w_out, hyb_w_in, hyb_w_out,
           conv_dw_w, conv_dw_b, conv_ln_g, conv_ln_b, gla_w_in, gla_w_gate, gla_b_gate,
           gla_norm_g, gla_w_out):
    b, s, d = x.shape
    m = b * s
    depth = norm_g.shape[0]
    sb_width = SB_HEADS * SB_HEAD_DIM
    conv_ch = conv_dw_w.shape[2]
    key_w = gla_w_gate.shape[2]
    val_w = gla_w_out.shape[1]

    h = x.reshape(m, d)
    for l in range(depth):
        g = norm_g[l][:, None, :]
        h = _ffn(h, g[0], g[1], ffn1_w_in[l].astype(BF16), ffn1_w_out[l].astype(BF16))
        if l % 2 == 0:
            e = l // 2
            qkv, cglu = _even_proj(h, g[2], hyb_w_in[e].astype(BF16), sb_width=sb_width, conv_ch=conv_ch)
            a = _sb_attention(qkv.reshape(b, s, 3 * sb_width), sb_width=sb_width)
            h = _even_out(a, cglu.reshape(b, s, conv_ch), h.reshape(b, s, d), conv_dw_w[e],
                          conv_dw_b[e][None, :], conv_ln_g[e][None, :], conv_ln_b[e][None, :],
                          hyb_w_out[e].astype(BF16), g[3]).reshape(m, d)
        else:
            o = l // 2
            q, k, v, r, lg = _odd_proj(
                h, g[2], gla_w_in[o].astype(BF16), gla_w_gate[o].astype(BF16), gla_b_gate[o][None, :],
                key_w=key_w, val_w=val_w)
            to3 = lambda a_: a_.reshape(b, s, a_.shape[-1])
            y = _gla(to3(q), to3(k), to3(v), to3(r), to3(lg), gla_norm_g[o][None, :])
            h = _odd_out(y.reshape(m, val_w), h, gla_w_out[o].astype(BF16), g[3])
        h = _ffn(h, g[4], g[5], ffn2_w_in[l].astype(BF16), ffn2_w_out[l].astype(BF16))
    return h.reshape(b, s, d)
```

```python
import functools

import jax
import jax.numpy as jnp
from jax import lax
from jax.experimental import pallas as pl
from jax.experimental.pallas import tpu as pltpu

F32 = jnp.float32
BF16 = jnp.bfloat16

NORM_EPS = 1e-6
FFN_RES = 0.5
SB_HEADS = 8
SB_HEAD_DIM = 64
CONV_WIDTH = 31
GLA_HEADS = 4
GATE_TAU = 16.0

LANES = 128
SUBLANES = 8
SB_TK = 128
SB_MASKED = -1e30
CONV_HALO = 32
GLA_SAFE_EXP = 80.0
VMEM_LIMIT = 56 * 1024 * 1024


def _params(*sem):
    return pltpu.CompilerParams(dimension_semantics=sem, vmem_limit_bytes=VMEM_LIMIT)


def _const_spec(shape):
    zeros = (0,) * len(shape)
    return pl.BlockSpec(shape, lambda *_: zeros, pipeline_mode=pl.Buffered(1))


def _rms(x, g):
    return x * lax.rsqrt(jnp.mean(x * x, axis=-1, keepdims=True) + NORM_EPS) * g


def _sigmoid(x):
    return 1.0 / (1.0 + jnp.exp(-x))


def _dot(a, b):
    return jnp.dot(a, b, preferred_element_type=F32)


def _dot_nt(a, b):
    return lax.dot_general(a, b, (((1,), (1,)), ((), ())), preferred_element_type=F32)


def _dot_tn(a, b):
    return lax.dot_general(a, b, (((0,), (0,)), ((), ())), preferred_element_type=F32)


def _ffn_kernel(h_ref, gpre_ref, gpost_ref, win_ref, wout_ref, o_ref, *, d_ff, n_chunks):
    _ffn_body(h_ref[...], gpre_ref, gpost_ref, win_ref, wout_ref, o_ref, d_ff, n_chunks)


def _mix_ffn_kernel(y_ref, h_ref, wmix_ref, gmix_ref, gpre_ref, gpost_ref, win_ref, wout_ref, o_ref,
                    *, d_ff, n_chunks):
    h = h_ref[...] + _rms(_dot(y_ref[...], wmix_ref[...]), gmix_ref[...])
    _ffn_body(h, gpre_ref, gpost_ref, win_ref, wout_ref, o_ref, d_ff, n_chunks)


def _ffn_body(h, gpre_ref, gpost_ref, win_ref, wout_ref, o_ref, d_ff, n_chunks):
    xn = _rms(h, gpre_ref[...]).astype(BF16)
    tf = d_ff // n_chunks
    f = None
    for c in range(n_chunks):
        a = _dot(xn, win_ref[:, c * tf:(c + 1) * tf])
        b = _dot(xn, win_ref[:, d_ff + c * tf:d_ff + (c + 1) * tf])
        s = (a * _sigmoid(a) * b).astype(BF16)
        part = _dot(s, wout_ref[c * tf:(c + 1) * tf, :])
        f = part if f is None else f + part
    o_ref[...] = h + FFN_RES * _rms(f, gpost_ref[...])


def _ffn(h, g_pre, g_post, w_in, w_out, *, tm=512, n_chunks=11):
    m, d = h.shape
    d_ff = w_out.shape[0]
    return pl.pallas_call(
        functools.partial(_ffn_kernel, d_ff=d_ff, n_chunks=n_chunks),
        grid=(m // tm,),
        in_specs=[
            pl.BlockSpec((tm, d), lambda i: (i, 0)),
            _const_spec((1, d)),
            _const_spec((1, d)),
            _const_spec((d, 2 * d_ff)),
            _const_spec((d_ff, d)),
        ],
        out_specs=pl.BlockSpec((tm, d), lambda i: (i, 0)),
        out_shape=jax.ShapeDtypeStruct((m, d), F32),
        compiler_params=_params("parallel"),
        name="ffn",
    )(h, g_pre, g_post, w_in, w_out)


def _mix_ffn(y, h, w_mix, g_mix, g_pre, g_post, w_in, w_out, *, tm=512, n_chunks=11):
    m, d = h.shape
    n = y.shape[1]
    d_ff = w_out.shape[0]
    return pl.pallas_call(
        functools.partial(_mix_ffn_kernel, d_ff=d_ff, n_chunks=n_chunks),
        grid=(m // tm,),
        in_specs=[
            pl.BlockSpec((tm, n), lambda i: (i, 0)),
            pl.BlockSpec((tm, d), lambda i: (i, 0)),
            _const_spec((n, d)),
            _const_spec((1, d)),
            _const_spec((1, d)),
            _const_spec((1, d)),
            _const_spec((d, 2 * d_ff)),
            _const_spec((d_ff, d)),
        ],
        out_specs=pl.BlockSpec((tm, d), lambda i: (i, 0)),
        out_shape=jax.ShapeDtypeStruct((m, d), F32),
        compiler_params=_params("parallel"),
        name="mix_ffn",
    )(y, h, w_mix, g_mix, g_pre, g_post, w_in, w_out)


def _even_proj_kernel(h_ref, g_ref, w_ref, qkv_ref, c_ref, *, sb_width, conv_ch):
    hn = _rms(h_ref[...], g_ref[...]).astype(BF16)
    scale = SB_HEAD_DIM ** -0.5
    qkv_ref[:, :sb_width] = (_dot(hn, w_ref[:, :sb_width]) * scale).astype(BF16)
    qkv_ref[:, sb_width:] = _dot(hn, w_ref[:, sb_width:3 * sb_width]).astype(BF16)
    u = _dot(hn, w_ref[:, 3 * sb_width:3 * sb_width + conv_ch])
    gate = _dot(hn, w_ref[:, 3 * sb_width + conv_ch:])
    c_ref[...] = u * _sigmoid(gate)


def _even_proj(h, g, w_in, *, sb_width, conv_ch, tm=512):
    m, d = h.shape
    n = w_in.shape[1]
    return pl.pallas_call(
        functools.partial(_even_proj_kernel, sb_width=sb_width, conv_ch=conv_ch),
        grid=(m // tm,),
        in_specs=[
            pl.BlockSpec((tm, d), lambda i: (i, 0)),
            _const_spec((1, d)),
            _const_spec((d, n)),
        ],
        out_specs=[
            pl.BlockSpec((tm, 3 * sb_width), lambda i: (i, 0)),
            pl.BlockSpec((tm, conv_ch), lambda i: (i, 0)),
        ],
        out_shape=[
            jax.ShapeDtypeStruct((m, 3 * sb_width), BF16),
            jax.ShapeDtypeStruct((m, conv_ch), F32),
        ],
        compiler_params=_params("parallel"),
        name="even_proj",
    )(h, g, w_in)


def _sb_attn_kernel(q_ref, k_ref, v_ref, bd_ref, o_ref, zr, zs, hs, ws, carry, acc):
    tq = q_ref.shape[1]
    tk = SB_TK
    nband = tq // tk
    i = pl.program_id(2)
    q0 = pl.multiple_of(i * tq, tq)
    q = q_ref[0]
    first_head = lax.broadcasted_iota(jnp.int32, (tk, LANES), 1) < SB_HEAD_DIM
    row = lax.broadcasted_iota(jnp.int32, (tq, 2 * tk), 0)
    key_in_block = lax.broadcasted_iota(jnp.int32, (tq, 2 * tk), 1) & (tk - 1)

    def key_start(n):
        return pl.multiple_of(q0 + (nband - 1 - n) * tk, tk)

    def stack_heads(blk):
        zero = jnp.zeros_like(blk)
        return jnp.concatenate([jnp.where(first_head, blk, zero), jnp.where(first_head, zero, blk)], axis=0)

    def score(n, slot, band):
        z = _dot_nt(q, stack_heads(k_ref[0, pl.ds(key_start(n), tk), :]))
        if band:
            z = jnp.where(key_in_block + (nband - 1 - n) * tk < row, z, SB_MASKED)
        zr[slot] = z

    def soft(slot):
        z = zr[slot]
        zs[slot] = z
        zb = z.astype(BF16)
        hs[slot] = jnp.maximum(zb, 0) + jnp.log(1 + jnp.exp(-jnp.abs(zb)))

    def weight(slot):
        bd = bd_ref[...]
        cum = _dot(hs[slot], bd)
        c = carry[...]
        ws[slot] = jnp.exp(zs[slot] + cum + c).astype(BF16)
        total = jnp.concatenate([jnp.broadcast_to(cum[:, 0:1], (tq, tk)),
                                 jnp.broadcast_to(cum[:, tk:tk + 1], (tq, tk))], axis=1)
        carry[...] = c + total

    def output(n, slot):
        acc[...] += _dot(ws[slot], stack_heads(v_ref[0, pl.ds(key_start(n), tk), :]))

    def step(n, slot, band):
        if not band or n >= 2:
            weight(slot)
        if not band or n >= 1:
            soft(1 - slot)
        if not band or n >= 3:
            output(n - 3, 1 - slot)
        score(n, slot, band)

    carry[...] = jnp.zeros_like(carry)
    acc[...] = jnp.zeros_like(acc)
    for n in range(nband):
        step(n, n % 2, True)

    def band_of_steps(m, _):
        for j in range(nband):
            step(nband * (m + 1) + j, j % 2, False)
        return 0

    lax.fori_loop(0, i, band_of_steps, 0)
    last = (i + 1) * nband
    output(last - 3, 1)
    weight(0)
    soft(1)
    output(last - 2, 0)
    weight(1)
    output(last - 1, 1)
    o_ref[0] = acc[...].astype(o_ref.dtype)


def _sb_attention(qkv, *, sb_width, tq=512):
    b, s, _ = qkv.shape
    tk = SB_TK
    assert tq % (2 * tk) == 0 and s % tq == 0
    pairs = sb_width // LANES
    tri = (jnp.arange(tk)[:, None] >= jnp.arange(tk)[None, :]).astype(F32)
    zero = jnp.zeros_like(tri)
    bd = -jnp.block([[tri, zero], [zero, tri]]).astype(BF16)
    return pl.pallas_call(
        _sb_attn_kernel,
        grid=(b, pairs, s // tq),
        in_specs=[
            pl.BlockSpec((1, tq, LANES), lambda bi, p, i: (bi, i, p)),
            pl.BlockSpec((1, s, LANES), lambda bi, p, i: (bi, 0, pairs + p)),
            pl.BlockSpec((1, s, LANES), lambda bi, p, i: (bi, 0, 2 * pairs + p)),
            _const_spec((2 * tk, 2 * tk)),
        ],
        out_specs=pl.BlockSpec((1, tq, LANES), lambda bi, p, i: (bi, i, p)),
        out_shape=jax.ShapeDtypeStruct((b, s, sb_width), BF16),
        scratch_shapes=[
            pltpu.VMEM((2, tq, 2 * tk), F32),
            pltpu.VMEM((2, tq, 2 * tk), F32),
            pltpu.VMEM((2, tq, 2 * tk), BF16),
            pltpu.VMEM((2, tq, 2 * tk), BF16),
            pltpu.VMEM((tq, 2 * tk), F32),
            pltpu.VMEM((tq, LANES), F32),
        ],
        compiler_params=_params("parallel", "parallel", "arbitrary"),
        name="sb_attention",
    )(qkv, qkv, qkv, bd)


def _even_out_kernel(a_ref, c_ref, halo_ref, h_ref, dww_ref, dwb_ref, lng_ref, lnb_ref,
                     wout_ref, g_ref, o_ref, xbuf, xshift, ybuf, *, rows):
    t = c_ref.shape[1]
    ch = c_ref.shape[2]
    i = pl.program_id(1)
    halo = halo_ref[0]
    xbuf[0:CONV_HALO, :] = jnp.where(i > 0, halo, jnp.zeros_like(halo))
    xbuf[CONV_HALO:, :] = c_ref[0]
    span = xshift.shape[1]
    for r in range(1, SUBLANES):
        xshift[r - 1] = xbuf[r:r + span, :]
    first = CONV_HALO - (CONV_WIDTH - 1)
    for r0 in range(0, t, rows):
        for c0 in range(0, ch, LANES):
            acc = jnp.zeros((rows, LANES), F32) + dwb_ref[:, c0:c0 + LANES]
            for k in range(CONV_WIDTH):
                r = (first + k) % SUBLANES
                base = first + k - r + r0
                if r == 0:
                    x = xbuf[base:base + rows, c0:c0 + LANES]
                else:
                    x = xshift[r - 1, base:base + rows, c0:c0 + LANES]
                acc = acc + dww_ref[k:k + 1, c0:c0 + LANES] * x
            ybuf[r0:r0 + rows, c0:c0 + LANES] = acc
    y = ybuf[...]
    mu = jnp.mean(y, axis=-1, keepdims=True)
    yc = y - mu
    var = jnp.mean(yc * yc, axis=-1, keepdims=True)
    yn = yc * lax.rsqrt(var + NORM_EPS) * lng_ref[...] + lnb_ref[...]
    cact = (yn * _sigmoid(yn)).astype(BF16)
    aw = a_ref.shape[2]
    m = _dot(a_ref[0], wout_ref[:aw, :]) + _dot(cact, wout_ref[aw:, :])
    o_ref[0] = h_ref[0] + _rms(m, g_ref[...])


def _even_out(a, c, h, dw_w, dw_b, ln_g, ln_b, w_out, g, *, t=256, rows=64):
    b, s, d = h.shape
    aw, ch = a.shape[2], c.shape[2]
    per = t // CONV_HALO
    return pl.pallas_call(
        functools.partial(_even_out_kernel, rows=rows),
        grid=(b, s // t),
        in_specs=[
            pl.BlockSpec((1, t, aw), lambda bi, i: (bi, i, 0)),
            pl.BlockSpec((1, t, ch), lambda bi, i: (bi, i, 0)),
            pl.BlockSpec((1, CONV_HALO, ch), lambda bi, i: (bi, jnp.maximum(i * per - 1, 0), 0)),
            pl.BlockSpec((1, t, d), lambda bi, i: (bi, i, 0)),
            _const_spec((CONV_WIDTH, ch)),
            _const_spec((1, ch)),
            _const_spec((1, ch)),
            _const_spec((1, ch)),
            _const_spec((aw + ch, d)),
            _const_spec((1, d)),
        ],
        out_specs=pl.BlockSpec((1, t, d), lambda bi, i: (bi, i, 0)),
        out_shape=jax.ShapeDtypeStruct((b, s, d), F32),
        scratch_shapes=[
            pltpu.VMEM((t + CONV_HALO, ch), F32),
            pltpu.VMEM((SUBLANES - 1, t + CONV_HALO - SUBLANES, ch), F32),
            pltpu.VMEM((t, ch), F32),
        ],
        compiler_params=_params("parallel", "parallel"),
        name="even_out",
    )(a, c, c, h, dw_w, dw_b, ln_g, ln_b, w_out, g)


def _odd_proj_kernel(h_ref, g_ref, w_ref, wgate_ref, bgate_ref,
                     q_ref, k_ref, v_ref, r_ref, lg_ref, *, key_w, val_w):
    hn = _rms(h_ref[...], g_ref[...]).astype(BF16)
    dk = key_w // GLA_HEADS
    g_lr = _dot(hn, w_ref[:, 2 * key_w + 2 * val_w:]).astype(BF16)
    q_ref[...] = _dot(hn, w_ref[:, :key_w]) * (dk ** -0.5)
    k_ref[...] = _dot(hn, w_ref[:, key_w:2 * key_w])
    v_ref[...] = _dot(hn, w_ref[:, 2 * key_w:2 * key_w + val_w]).astype(BF16)
    r_ref[...] = _dot(hn, w_ref[:, 2 * key_w + val_w:2 * key_w + 2 * val_w])
    x = _dot(g_lr, wgate_ref[...]) + bgate_ref[...]
    lg_ref[...] = -(jnp.maximum(-x, 0.0) + jnp.log(1.0 + jnp.exp(-jnp.abs(x)))) / GATE_TAU


def _odd_proj(h, g, w_in, w_gate, b_gate, *, key_w, val_w, tm=512):
    m, d = h.shape
    rank = w_gate.shape[0]
    row = lambda i: (i, 0)
    return pl.pallas_call(
        functools.partial(_odd_proj_kernel, key_w=key_w, val_w=val_w),
        grid=(m // tm,),
        in_specs=[
            pl.BlockSpec((tm, d), row),
            _const_spec((1, d)),
            _const_spec((d, 2 * key_w + 2 * val_w + rank)),
            _const_spec((rank, key_w)),
            _const_spec((1, key_w)),
        ],
        out_specs=[
            pl.BlockSpec((tm, key_w), row),
            pl.BlockSpec((tm, key_w), row),
            pl.BlockSpec((tm, val_w), row),
            pl.BlockSpec((tm, val_w), row),
            pl.BlockSpec((tm, key_w), row),
        ],
        out_shape=[
            jax.ShapeDtypeStruct((m, key_w), F32),
            jax.ShapeDtypeStruct((m, key_w), F32),
            jax.ShapeDtypeStruct((m, val_w), BF16),
            jax.ShapeDtypeStruct((m, val_w), F32),
            jax.ShapeDtypeStruct((m, key_w), F32),
        ],
        compiler_params=_params("parallel"),
        name="odd_proj",
    )(h, g, w_in, w_gate, b_gate)


def _gla_kernel(q_ref, k_ref, v_ref, r_ref, lg_ref, tri_ref, ng_ref, o_ref,
                state_ref, sc_ref):
    nb, c, key_w = q_ref.shape
    dk = key_w // GLA_HEADS
    dv = v_ref.shape[2] // GLA_HEADS

    @pl.when(pl.program_id(0) == 0)
    def _():
        state_ref[...] = jnp.zeros_like(state_ref)

    tri = tri_ref[...]
    bs, mids, lasts = [], [], []
    spread = None
    for bi in range(nb):
        lg = lg_ref[bi]
        g1 = lg.astype(BF16)
        rem = lg - g1.astype(F32)
        g2 = rem.astype(BF16)
        g3 = (rem - g2.astype(F32)).astype(BF16)
        b = _dot(tri, g1) + _dot(tri, g2) + _dot(tri, g3)
        mid = b[c // 2:c // 2 + 1, :]
        bs.append(b)
        mids.append(mid)
        lasts.append(b[c - 1:c, :])
        dev = jnp.abs(b - mid)
        spread = dev if spread is None else jnp.maximum(spread, dev)
    safe = jnp.max(spread) < GLA_SAFE_EXP

    row = lax.broadcasted_iota(jnp.int32, (c, c), 0)
    col = lax.broadcasted_iota(jnp.int32, (c, c), 1)
    causal = col <= row
    row_k = lax.broadcasted_iota(jnp.int32, (c, dk), 0)

    def rebuilt_scores(q, k, bh):
        sc_ref[...] = jnp.zeros_like(sc_ref)

        def column(s, carry):
            is_s = row_k == s
            k_s = jnp.sum(jnp.where(is_s, k, 0.0), axis=0, keepdims=True)
            b_s = jnp.sum(jnp.where(is_s, bh, 0.0), axis=0, keepdims=True)
            decay = jnp.exp(jnp.minimum(bh - b_s, 0.0))
            val = jnp.sum(q * k_s * decay, axis=-1, keepdims=True)
            sc_ref[...] += jnp.where(col == s, val, 0.0)
            return carry

        lax.fori_loop(0, c, column, 0)
        return sc_ref[...]

    def all_heads(factored):
        for bi in range(nb):
            for hd in range(GLA_HEADS):
                ks = slice(hd * dk, (hd + 1) * dk)
                vs = slice(hd * dv, (hd + 1) * dv)
                q = q_ref[bi, :, ks]
                k = k_ref[bi, :, ks]
                v = v_ref[bi, :, vs]
                bh = bs[bi][:, ks]
                mh = mids[bi][:, ks]
                lh = lasts[bi][:, ks]
                if factored:
                    qd = (q * jnp.exp(bh - mh)).astype(BF16)
                    kd = (k * jnp.exp(mh - bh)).astype(BF16)
                    raw = _dot_nt(qd, kd)
                else:
                    raw = rebuilt_scores(q, k, bh)
                scores = jnp.where(causal, raw, 0.0).astype(BF16)
                st = state_ref[bi, hd]
                q_in = (q * jnp.exp(bh)).astype(BF16)
                o = _dot(scores, v) + _dot_nt(q_in, st.astype(BF16))
                k_out = (k * jnp.exp(lh - bh)).astype(BF16)
                state_ref[bi, hd] = st * jnp.exp(lh) + _dot_tn(v, k_out)
                on = _rms(o, ng_ref[...])
                r = r_ref[bi, :, vs]
                o_ref[bi, :, vs] = (on * (r * _sigmoid(r))).astype(o_ref.dtype)

    pl.when(safe)(lambda: all_heads(True))
    pl.when(jnp.logical_not(safe))(lambda: all_heads(False))


def _gla(q, k, v, r, lg, norm_g, *, c=128):
    b, s, key_w = q.shape
    val_w = v.shape[2]
    dk, dv = key_w // GLA_HEADS, val_w // GLA_HEADS
    tri = (jnp.arange(c)[:, None] >= jnp.arange(c)[None, :]).astype(BF16)
    blk = lambda w: pl.BlockSpec((b, c, w), lambda i: (0, i, 0))
    return pl.pallas_call(
        _gla_kernel,
        grid=(s // c,),
        in_specs=[blk(key_w), blk(key_w), blk(val_w), blk(val_w), blk(key_w),
                  _const_spec((c, c)), _const_spec((1, dv))],
        out_specs=blk(val_w),
        out_shape=jax.ShapeDtypeStruct((b, s, val_w), BF16),
        scratch_shapes=[
            pltpu.VMEM((b, GLA_HEADS, dv, dk), F32),
            pltpu.VMEM((c, c), F32),
        ],
        compiler_params=_params("arbitrary"),
        name="gla",
    )(q, k, v, r, lg, tri, norm_g)


def kernel(x, norm_g, ffn1_w_in, ffn1_w_out, ffn2_w_in, ffn2_w_out, hyb_w_in, hyb_w_out,
           conv_dw_w, conv_dw_b, conv_ln_g, conv_ln_b, gla_w_in, gla_w_gate, gla_b_gate,
           gla_norm_g, gla_w_out):
    b, s, d = x.shape
    m = b * s
    depth = norm_g.shape[0]
    sb_width = SB_HEADS * SB_HEAD_DIM
    conv_ch = conv_dw_w.shape[2]
    key_w = gla_w_gate.shape[2]
    val_w = gla_w_out.shape[1]

    h = x.reshape(m, d)
    for l in range(depth):
        g = norm_g[l][:, None, :]
        h = _ffn(h, g[0], g[1], ffn1_w_in[l].astype(BF16), ffn1_w_out[l].astype(BF16))
        if l % 2 == 0:
            e = l // 2
            qkv, cglu = _even_proj(h, g[2], hyb_w_in[e].astype(BF16), sb_width=sb_width, conv_ch=conv_ch)
            a = _sb_attention(qkv.reshape(b, s, 3 * sb_width), sb_width=sb_width)
            h = _even_out(a, cglu.reshape(b, s, conv_ch), h.reshape(b, s, d), conv_dw_w[e],
                          conv_dw_b[e][None, :], conv_ln_g[e][None, :], conv_ln_b[e][None, :],
                          hyb_w_out[e].astype(BF16), g[3]).reshape(m, d)
            h = _ffn(h, g[4], g[5], ffn2_w_in[l].astype(BF16), ffn2_w_out[l].astype(BF16))
        else:
            o = l // 2
            q, k, v, r, lg = _odd_proj(
                h, g[2], gla_w_in[o].astype(BF16), gla_w_gate[o].astype(BF16), gla_b_gate[o][None, :],
                key_w=key_w, val_w=val_w)
            to3 = lambda a_: a_.reshape(b, s, a_.shape[-1])
            y = _gla(to3(q), to3(k), to3(v), to3(r), to3(lg), gla_norm_g[o][None, :])
            h = _mix_ffn(y.reshape(m, val_w), h, gla_w_out[o].astype(BF16), g[3], g[4], g[5],
                         ffn2_w_in[l].astype(BF16), ffn2_w_out[l].astype(BF16))
    return h.reshape(b, s, d)
```
